```python
import numpy as np
import jax, jax.numpy as jnp
from jax import lax

D_MODEL = 1024
BATCH = 8
SEQ = 4096
DEPTH = 4

ROPE_THETA = 500000.0
NORM_EPS = 1e-6
Q_BLOCK = 128
MLA_HEADS = 8
MLA_NOPE_DIM = 64
MLA_ROPE_DIM = 32
MLA_QK_DIM = MLA_NOPE_DIM + MLA_ROPE_DIM
MLA_V_DIM = 64
MLA_Q_RANK = 384
MLA_KV_RANK = 256
CONV_CH = 512
CONV_WIDTH = 31
NSA_HEADS = 8
NSA_KV_GROUPS = 2
NSA_HEAD_DIM = 64
NSA_ROPE_DIM = NSA_HEAD_DIM // 4
NSA_N_BRANCH = 3
CMP_BLOCK = 32
CMP_STRIDE = 16
CMP_HIDDEN = 128
SLC_BLOCK = 64
SLC_TOP_N = 16
WINDOW = 512
NSA_Q_BLOCK = 64
FORCED_BLOCK_SCORE = 1e6
N_MIXERS = 3
D_FF = 4 * D_MODEL

IN_SPLITS = (
    MLA_Q_RANK,
    MLA_KV_RANK,
    MLA_ROPE_DIM,
    2 * CONV_CH,
    NSA_HEADS * NSA_HEAD_DIM,
    2 * NSA_N_BRANCH * NSA_KV_GROUPS * NSA_HEAD_DIM,
    NSA_N_BRANCH * NSA_HEADS,
    N_MIXERS * D_MODEL,
)
D_IN = sum(IN_SPLITS)
SPLIT_IDX = tuple(int(v) for v in np.cumsum(IN_SPLITS)[:-1])

kernel_name = "hybrid_mla_conformer_nsa_block"


def rms_norm(x, g):
    xf = x.astype(jnp.float32)
    y = xf * lax.rsqrt(jnp.mean(xf * xf, axis=-1, keepdims=True) + NORM_EPS)
    return (y * g.astype(jnp.float32)).astype(x.dtype)


def layer_norm(x, g, b):
    xf = x.astype(jnp.float32)
    mu = jnp.mean(xf, axis=-1, keepdims=True)
    var = jnp.mean(jnp.square(xf - mu), axis=-1, keepdims=True)
    y = (xf - mu) * lax.rsqrt(var + NORM_EPS)
    return (y * g.astype(jnp.float32) + b.astype(jnp.float32)).astype(x.dtype)


def rope(x, pos, rot_dim):
    half = rot_dim // 2
    inv = jnp.power(jnp.float32(ROPE_THETA), -jnp.arange(half, dtype=jnp.float32) * (2.0 / rot_dim))
    ang = pos.astype(jnp.float32)[:, None] * inv
    cos = jnp.cos(ang)[:, None, :]
    sin = jnp.sin(ang)[:, None, :]
    xf = x.astype(jnp.float32)
    x1, x2 = xf[..., :half], xf[..., half:rot_dim]
    out = jnp.concatenate([x1 * cos - x2 * sin, x2 * cos + x1 * sin, xf[..., rot_dim:]], axis=-1)
    return out.astype(x.dtype)


def masked_softmax(s, mask):
    s = jnp.where(mask, s.astype(jnp.float32), -jnp.inf)
    m = jnp.max(s, axis=-1, keepdims=True)
    m = jnp.where(jnp.isfinite(m), m, 0.0)
    p = jnp.exp(s - m)
    return p / jnp.maximum(jnp.sum(p, axis=-1, keepdims=True), 1e-30)


def mla_mixer(c_q, c_kv, k_rope, g_cq, g_ckv, w_uq, w_ukv, g_q, g_k, w_o, pos):
    B, S, _ = c_q.shape
    dt = c_q.dtype
    q = (rms_norm(c_q, g_cq) @ w_uq).reshape(B, S, MLA_HEADS, MLA_QK_DIM)
    kv = (rms_norm(c_kv, g_ckv) @ w_ukv).reshape(B, S, MLA_HEADS, MLA_NOPE_DIM + MLA_V_DIM)
    k_nope, v = kv[..., :MLA_NOPE_DIM], kv[..., MLA_NOPE_DIM:]
    k_r = jnp.broadcast_to(k_rope[:, :, None, :], (B, S, MLA_HEADS, MLA_ROPE_DIM))
    k = jnp.concatenate([k_r, k_nope], axis=-1)
    q = rope(rms_norm(q, g_q), pos, MLA_ROPE_DIM)
    k = rope(rms_norm(k, g_k), pos, MLA_ROPE_DIM)
    scale = MLA_QK_DIM ** -0.5

    def block(i):
        qb = lax.dynamic_slice_in_dim(q, i * Q_BLOCK, Q_BLOCK, axis=1)
        t = i * Q_BLOCK + jnp.arange(Q_BLOCK)
        s = jnp.einsum('bqhd,bkhd->bhqk', qb, k) * scale
        p = masked_softmax(s, pos[None, :] <= t[:, None])
        return jnp.einsum('bhqk,bkhd->bqhd', p.astype(dt), v)

    o = lax.map(block, jnp.arange(S // Q_BLOCK))
    o = jnp.moveaxis(o, 0, 1).reshape(B, S, MLA_HEADS * MLA_V_DIM)
    return o @ w_o


def conv_mixer(u2, b_glu, w_dw, b_dw, g_ln, b_ln, w_out, b_out):
    a, gate = jnp.split(u2 + b_glu, 2, axis=-1)
    u = a * jax.nn.sigmoid(gate)
    u = lax.conv_general_dilated(u, w_dw[:, None, :], (1,), [(CONV_WIDTH - 1, 0)],
                                 dimension_numbers=('NWC', 'WIO', 'NWC'),
                                 feature_group_count=CONV_CH) + b_dw
    u = jax.nn.silu(layer_norm(u, g_ln, b_ln))
    return u @ w_out + b_out


def compress(tok, pe, w1, w2):
    B, S, G, dh = tok.shape
    n_cmp = (S - CMP_BLOCK) // CMP_STRIDE + 1
    idx = (jnp.arange(n_cmp) * CMP_STRIDE)[:, None] + jnp.arange(CMP_BLOCK)[None, :]
    blocks = tok[:, idx] + pe[None, None, :, None, :]
    blocks = jnp.moveaxis(blocks, 2, 3).reshape(B, n_cmp, G, CMP_BLOCK * dh)
    return jax.nn.silu(blocks @ w1) @ w2


def nsa_mixer(q_raw, kv_raw, gate_logits, pe_k, pe_v, w_ck1, w_ck2, w_cv1, w_cv2, g_q, g_k, w_o, pos):
    B, S, _ = q_raw.shape
    dt = q_raw.dtype
    H, G, dh = NSA_HEADS, NSA_KV_GROUPS, NSA_HEAD_DIM
    hpg = H // G
    QB = NSA_Q_BLOCK
    q = rope(rms_norm(q_raw.reshape(B, S, H, dh), g_q), pos, NSA_ROPE_DIM)
    kv = kv_raw.reshape(B, S, 2 * NSA_N_BRANCH, G, dh)
    kc, vc, ks, vs, kw, vw = [kv[:, :, j] for j in range(2 * NSA_N_BRANCH)]
    n_cmp = (S - CMP_BLOCK) // CMP_STRIDE + 1
    cmp_pos = jnp.arange(n_cmp) * CMP_STRIDE + CMP_BLOCK - 1
    k_cmp = rope(rms_norm(compress(kc, pe_k, w_ck1, w_ck2), g_k), cmp_pos, NSA_ROPE_DIM)
    v_cmp = compress(vc, pe_v, w_cv1, w_cv2)
    n_slc = S // SLC_BLOCK
    top_n = min(SLC_TOP_N, n_slc)
    k_slc = rope(rms_norm(ks, g_k), pos, NSA_ROPE_DIM)
    k_slc_blk = k_slc.reshape(B, n_slc, SLC_BLOCK, G, dh).transpose(0, 3, 1, 2, 4)
    v_slc_blk = vs.reshape(B, n_slc, SLC_BLOCK, G, dh).transpose(0, 3, 1, 2, 4)
    c_start = jnp.arange(n_cmp) * CMP_STRIDE
    s_start = jnp.arange(n_slc) * SLC_BLOCK
    overlap = ((c_start[:, None] < s_start[None, :] + SLC_BLOCK)
               & (c_start[:, None] + CMP_BLOCK > s_start[None, :])).astype(jnp.float32)
    k_win = rope(rms_norm(kw, g_k), pos, NSA_ROPE_DIM)
    k_win_pad = jnp.pad(k_win, ((0, 0), (WINDOW, 0), (0, 0), (0, 0)))
    v_win_pad = jnp.pad(vw, ((0, 0), (WINDOW, 0), (0, 0), (0, 0)))
    scale = dh ** -0.5
    bi = jnp.arange(B)[:, None, None, None]
    gi = jnp.arange(G)[None, :, None, None]
    j_blk = jnp.arange(n_slc)

    def block(i):
        q0 = i * QB
        t = q0 + jnp.arange(QB)
        qb = lax.dynamic_slice_in_dim(q, q0, QB, axis=1).reshape(B, QB, G, hpg, dh)
        s_c = jnp.einsum('bqghd,bcgd->bghqc', qb, k_cmp) * scale
        p_c = masked_softmax(s_c, cmp_pos[None, :] <= t[:, None])
        o_c = jnp.einsum('bghqc,bcgd->bqghd', p_c.astype(dt), v_cmp)
        imp = jnp.einsum('bghqc,cn->bgqn', p_c, overlap)
        blk_t = t // SLC_BLOCK
        forced = ((j_blk[None, :] == 0) | (j_blk[None, :] == blk_t[:, None])
                  | (j_blk[None, :] == blk_t[:, None] - 1))
        imp = jnp.where(forced, FORCED_BLOCK_SCORE, imp)
        imp = jnp.where(j_blk[None, :] > blk_t[:, None], -jnp.inf, imp)
        _, idx = lax.top_k(imp, top_n)
        k_sel = k_slc_blk[bi, gi, idx]
        v_sel = v_slc_blk[bi, gi, idx]
        s_s = jnp.einsum('bqghd,bgqnkd->bghqnk', qb, k_sel) * scale
        kpos = idx[..., None] * SLC_BLOCK + jnp.arange(SLC_BLOCK)
        m_s = (kpos <= t[None, None, :, None, None])[:, :, None]
        p_s = masked_softmax(s_s.reshape(B, G, hpg, QB, top_n * SLC_BLOCK),
                             m_s.reshape(B, G, 1, QB, top_n * SLC_BLOCK))
        o_s = jnp.einsum('bghqnk,bgqnkd->bqghd', p_s.reshape(s_s.shape).astype(dt), v_sel)
        k_w = lax.dynamic_slice_in_dim(k_win_pad, q0, WINDOW + QB, axis=1)
        v_w = lax.dynamic_slice_in_dim(v_win_pad, q0, WINDOW + QB, axis=1)
        wpos = q0 - WINDOW + jnp.arange(WINDOW + QB)
        m_w = ((wpos[None, :] <= t[:, None]) & (wpos[None, :] > t[:, None] - WINDOW)
               & (wpos[None, :] >= 0))
        s_w = jnp.einsum('bqghd,bkgd->bghqk', qb, k_w) * scale
        p_w = masked_softmax(s_w, m_w)
        o_w = jnp.einsum('bghqk,bkgd->bqghd', p_w.astype(dt), v_w)
        return jnp.stack([o_c, o_s, o_w], axis=-2)

    o = lax.map(block, jnp.arange(S // QB))
    o = jnp.moveaxis(o, 0, 1).reshape(B, S, H, NSA_N_BRANCH, dh)
    g = jax.nn.sigmoid(gate_logits.reshape(B, S, H, NSA_N_BRANCH).astype(jnp.float32)).astype(dt)
    o = jnp.einsum('bshnd,bshn->bshd', o, g).reshape(B, S, H * dh)
    return o @ w_o


def hybrid_layer(x, pos, g_mix, w_in, g_cq, g_ckv, w_uq, w_ukv, g_q_mla, g_k_mla, w_o_mla,
                 b_glu, w_dw, b_dw, g_conv_ln, b_conv_ln, w_conv_out, b_conv_out,
                 pe_cmp_k, pe_cmp_v, w_cmp_k1, w_cmp_k2, w_cmp_v1, w_cmp_v2,
                 g_q_nsa, g_k_nsa, w_o_nsa, w_out, g_ffn, w_ff1, w_ff2):
    dt = x.dtype
    h = rms_norm(x, g_mix)
    z = h @ w_in
    c_q, c_kv, k_rope, u2, q_nsa, kv_nsa, gate_nsa, gate_mix = jnp.split(z, SPLIT_IDX, axis=-1)
    o_a = mla_mixer(c_q, c_kv, k_rope, g_cq, g_ckv, w_uq, w_ukv, g_q_mla, g_k_mla, w_o_mla, pos)
    o_b = conv_mixer(u2, b_glu, w_dw, b_dw, g_conv_ln, b_conv_ln, w_conv_out, b_conv_out)
    o_c = nsa_mixer(q_nsa, kv_nsa, gate_nsa, pe_cmp_k, pe_cmp_v, w_cmp_k1, w_cmp_k2,
                    w_cmp_v1, w_cmp_v2, g_q_nsa, g_k_nsa, w_o_nsa, pos)
    g_a, g_b, g_c = jnp.split(jax.nn.sigmoid(gate_mix.astype(jnp.float32)).astype(dt), N_MIXERS, axis=-1)
    x = x + (g_a * o_a + g_b * o_b + g_c * o_c) @ w_out
    h = rms_norm(x, g_ffn)
    x = x + jnp.square(jax.nn.relu(h @ w_ff1)) @ w_ff2
    return x


def setup_inputs(seed: int = 0) -> dict:
    key = jax.random.key(seed)
    ks = iter(jax.random.split(key, 40))
    L = DEPTH
    res = (2.0 * DEPTH) ** -0.5

    def nrm(shape, scale):
        return jax.random.normal(next(ks), shape, jnp.float32) * scale

    def gain(shape):
        return 1.0 + nrm(shape, 0.02)

    return {
        "x": nrm((BATCH, SEQ, D_MODEL), 1.0),
        "g_mix": gain((L, D_MODEL)),
        "w_in": nrm((L, D_MODEL, D_IN), D_MODEL ** -0.5),
        "g_cq": gain((L, MLA_Q_RANK)),
        "g_ckv": gain((L, MLA_KV_RANK)),
        "w_uq": nrm((L, MLA_Q_RANK, MLA_HEADS * MLA_QK_DIM), MLA_Q_RANK ** -0.5),
        "w_ukv": nrm((L, MLA_KV_RANK, MLA_HEADS * (MLA_NOPE_DIM + MLA_V_DIM)), MLA_KV_RANK ** -0.5),
        "g_q_mla": gain((L, MLA_QK_DIM)),
        "g_k_mla": gain((L, MLA_QK_DIM)),
        "w_o_mla": nrm((L, MLA_HEADS * MLA_V_DIM, D_MODEL), (MLA_HEADS * MLA_V_DIM) ** -0.5),
        "b_glu": nrm((L, 2 * CONV_CH), 0.01),
        "w_dw": nrm((L, CONV_WIDTH, CONV_CH), CONV_WIDTH ** -0.5),
        "b_dw": nrm((L, CONV_CH), 0.01),
        "g_conv_ln": gain((L, CONV_CH)),
        "b_conv_ln": nrm((L, CONV_CH), 0.01),
        "w_conv_out": nrm((L, CONV_CH, D_MODEL), CONV_CH ** -0.5),
        "b_conv_out": nrm((L, D_MODEL), 0.01),
        "pe_cmp_k": nrm((L, CMP_BLOCK, NSA_HEAD_DIM), 0.1),
        "pe_cmp_v": nrm((L, CMP_BLOCK, NSA_HEAD_DIM), 0.1),
        "w_cmp_k1": nrm((L, CMP_BLOCK * NSA_HEAD_DIM, CMP_HIDDEN), (CMP_BLOCK * NSA_HEAD_DIM) ** -0.5),
        "w_cmp_k2": nrm((L, CMP_HIDDEN, NSA_HEAD_DIM), CMP_HIDDEN ** -0.5),
        "w_cmp_v1": nrm((L, CMP_BLOCK * NSA_HEAD_DIM, CMP_HIDDEN), (CMP_BLOCK * NSA_HEAD_DIM) ** -0.5),
        "w_cmp_v2": nrm((L, CMP_HIDDEN, NSA_HEAD_DIM), CMP_HIDDEN ** -0.5),
        "g_q_nsa": gain((L, NSA_HEAD_DIM)),
        "g_k_nsa": gain((L, NSA_HEAD_DIM)),
        "w_o_nsa": nrm((L, NSA_HEADS * NSA_HEAD_DIM, D_MODEL), (NSA_HEADS * NSA_HEAD_DIM) ** -0.5),
        "w_out": nrm((L, D_MODEL, D_MODEL), D_MODEL ** -0.5 * res),
        "g_ffn": gain((L, D_MODEL)),
        "w_ff1": nrm((L, D_MODEL, D_FF), D_MODEL ** -0.5),
        "w_ff2": nrm((L, D_FF, D_MODEL), D_FF ** -0.5 * res),
    }


def reference(x, g_mix, w_in, g_cq, g_ckv, w_uq, w_ukv, g_q_mla, g_k_mla, w_o_mla,
              b_glu, w_dw, b_dw, g_conv_ln, b_conv_ln, w_conv_out, b_conv_out,
              pe_cmp_k, pe_cmp_v, w_cmp_k1, w_cmp_k2, w_cmp_v1, w_cmp_v2,
              g_q_nsa, g_k_nsa, w_o_nsa, w_out, g_ffn, w_ff1, w_ff2):
    pos = jnp.arange(x.shape[1])
    for l in range(DEPTH):
        x = hybrid_layer(x, pos, g_mix[l], w_in[l], g_cq[l], g_ckv[l], w_uq[l], w_ukv[l],
                         g_q_mla[l], g_k_mla[l], w_o_mla[l], b_glu[l], w_dw[l], b_dw[l],
                         g_conv_ln[l], b_conv_ln[l], w_conv_out[l], b_conv_out[l],
                         pe_cmp_k[l], pe_cmp_v[l], w_cmp_k1[l], w_cmp_k2[l], w_cmp_v1[l], w_cmp_v2[l],
                         g_q_nsa[l], g_k_nsa[l], w_o_nsa[l], w_out[l], g_ffn[l], w_ff1[l], w_ff2[l])
    return x
```

```python
import functools

import numpy as np
import jax
import jax.numpy as jnp
from jax import lax
from jax.experimental import pallas as pl
from jax.experimental.pallas import tpu as pltpu

F32 = jnp.float32
BF16 = jnp.bfloat16

ROPE_THETA = 500000.0
NORM_EPS = 1e-6
MLA_HEADS = 8
MLA_NOPE_DIM = 64
MLA_ROPE_DIM = 32
MLA_QK_DIM = MLA_NOPE_DIM + MLA_ROPE_DIM
MLA_V_DIM = 64
CONV_CH = 512
CONV_WIDTH = 31
NSA_HEADS = 8
NSA_KV_GROUPS = 2
NSA_HEAD_DIM = 64
NSA_ROPE_DIM = NSA_HEAD_DIM // 4
NSA_HPG = NSA_HEADS // NSA_KV_GROUPS
CMP_BLOCK = 32
CMP_STRIDE = 16
CMP_HIDDEN = 128
SLC_BLOCK = 64
SLC_TOP_N = 16
WINDOW = 512
FORCED_BLOCK_SCORE = 1e6

LANES = 128
SUBLANES = 8
VMEM_LIMIT_BYTES = 56 * 1024 * 1024

MASK_VALUE = -1e30

TM_PROJ = 256
TM_PREP = 512
TQ_MLA = 512
TS_CONV = 256
CONV_HALO = 32
CONV_CHUNK = 64
TQ_NSA = 256
TM_MERGE = 512
TM_FFN = 1024
TF_FFN = 512


def _dot(a, b):
    return jnp.dot(a, b, preferred_element_type=F32)


def _dot_nt(a, b):
    return lax.dot_general(a, b, (((1,), (1,)), ((), ())), preferred_element_type=F32)


def _rms(x, g):
    return x * lax.rsqrt(jnp.mean(x * x, axis=-1, keepdims=True) + NORM_EPS) * g


def _params(*sem):
    return pltpu.CompilerParams(dimension_semantics=sem, vmem_limit_bytes=VMEM_LIMIT_BYTES)


def _flash_init(m_ref, l_ref, acc_ref):
    m_ref[...] = jnp.full(m_ref.shape, MASK_VALUE, F32)
    l_ref[...] = jnp.zeros(l_ref.shape, F32)
    acc_ref[...] = jnp.zeros(acc_ref.shape, F32)


def _flash_step(q, k, v, m_ref, l_ref, acc_ref, mask):
    s = _dot_nt(q, k)
    if mask is not None:
        s = jnp.where(mask, s, MASK_VALUE)
    m_prev = m_ref[...]
    m_new = jnp.maximum(m_prev, jnp.max(s, axis=-1, keepdims=True))
    alpha = jnp.exp(m_prev - m_new)
    p = jnp.exp(s - m_new)
    l_ref[...] = alpha * l_ref[...] + jnp.sum(p, axis=-1, keepdims=True)
    acc_ref[...] = alpha * acc_ref[...] + _dot(p.astype(BF16), v)
    m_ref[...] = m_new


def _inproj_kernel(x_ref, g_ref, w_ref, *out_refs, widths):
    h = _rms(x_ref[...], g_ref[...]).astype(BF16)
    off = 0
    for o_ref, wd in zip(out_refs, widths):
        o_ref[...] = _dot(h, w_ref[:, off:off + wd]).astype(o_ref.dtype)
        off += wd


def _inproj(x2d, g, w, widths):
    t, d = x2d.shape
    n = w.shape[1]
    return pl.pallas_call(
        functools.partial(_inproj_kernel, widths=widths),
        out_shape=[jax.ShapeDtypeStruct((t, wd), F32) for wd in widths],
        grid=(t // TM_PROJ,),
        in_specs=[pl.BlockSpec((TM_PROJ, d), lambda i: (i, 0)),
                  pl.BlockSpec((1, d), lambda i: (0, 0)),
                  pl.BlockSpec((d, n), lambda i: (0, 0), pipeline_mode=pl.Buffered(1))],
        out_specs=[pl.BlockSpec((TM_PROJ, wd), lambda i: (i, 0)) for wd in widths],
        compiler_params=_params("parallel"),
        name="inproj",
    )(x2d, g, w)


def _mla_prep_kernel(cq_ref, ckv_ref, kr_ref, cos_ref, sin_ref, gcq_ref, gckv_ref,
                     wuq_ref, wuk_ref, wuv_ref, gq_ref, gk_ref, q_ref, k_ref, v_ref):
    cos = cos_ref[...]
    sin = sin_ref[...]
    half = MLA_ROPE_DIM // 2
    first = lax.broadcasted_iota(jnp.int32, cos.shape, 1) < half

    def rope(x):
        rot = jnp.where(first, pltpu.roll(x, LANES - half, 1), pltpu.roll(x, half, 1))
        return x * cos + rot * sin

    inv_dim = 1.0 / MLA_QK_DIM
    q = _dot(_rms(cq_ref[...], gcq_ref[...]).astype(BF16), wuq_ref[...])
    gq = gq_ref[...]
    for h in range(MLA_HEADS):
        qh = q[:, h * LANES:(h + 1) * LANES]
        r = lax.rsqrt(jnp.sum(qh * qh, axis=-1, keepdims=True) * inv_dim + NORM_EPS)
        q_ref[0, h] = rope(qh * r * gq).astype(q_ref.dtype)

    ckvn = _rms(ckv_ref[...], gckv_ref[...]).astype(BF16)
    kn = _dot(ckvn, wuk_ref[...])
    v = _dot(ckvn, wuv_ref[...])
    kr = kr_ref[...]
    gk = gk_ref[...]
    for h in range(MLA_HEADS):
        kh = kn[:, h * LANES:(h + 1) * LANES] + kr
        r = lax.rsqrt(jnp.sum(kh * kh, axis=-1, keepdims=True) * inv_dim + NORM_EPS)
        k_ref[0, h] = rope(kh * r * gk).astype(k_ref.dtype)
        v_ref[0, h] = v[:, h * MLA_V_DIM:(h + 1) * MLA_V_DIM].astype(v_ref.dtype)


def _mla_prep(cq, ckv, kr, cos, sin, gcq, gckv, wuq, wuk, wuv, gq, gk, b, s):
    tm = min(TM_PREP, s)
    nb = s // tm
    row = lambda width: pl.BlockSpec((tm, width), lambda bi, i: (bi * nb + i, 0))
    tab = pl.BlockSpec((tm, LANES), lambda bi, i: (i, 0))
    full = lambda a: pl.BlockSpec(a.shape, lambda bi, i: (0,) * a.ndim)
    hd = lambda width: pl.BlockSpec((1, MLA_HEADS, tm, width), lambda bi, i: (bi, 0, i, 0))
    return pl.pallas_call(
        _mla_prep_kernel,
        out_shape=[jax.ShapeDtypeStruct((b, MLA_HEADS, s, LANES), BF16),
                   jax.ShapeDtypeStruct((b, MLA_HEADS, s, LANES), BF16),
                   jax.ShapeDtypeStruct((b, MLA_HEADS, s, MLA_V_DIM), BF16)],
        grid=(b, nb),
        in_specs=[row(cq.shape[1]), row(ckv.shape[1]), row(LANES), tab, tab,
                  full(gcq), full(gckv), full(wuq), full(wuk), full(wuv), full(gq), full(gk)],
        out_specs=[hd(LANES), hd(LANES), hd(MLA_V_DIM)],
        compiler_params=_params("parallel", "parallel"),
        name="mla_prep",
    )(cq, ckv, kr, cos, sin, gcq, gckv, wuq, wuk, wuv, gq, gk)


def _mla_attn_kernel(q_ref, k_ref, v_ref, o_ref, m_ref, l_ref, acc_ref, *, tq):
    i = pl.program_id(2)
    row = lax.broadcasted_iota(jnp.int32, (tq, tq), 0)
    col = lax.broadcasted_iota(jnp.int32, (tq, tq), 1)
    causal = col <= row
    for h in range(2):
        q = q_ref[0, h]
        _flash_init(m_ref, l_ref, acc_ref)

        def body(j, carry, h=h, q=q):
            start = pl.multiple_of(j * tq, tq)
            _flash_step(q, k_ref[0, h, pl.ds(start, tq), :], v_ref[0, h, pl.ds(start, tq), :],
                        m_ref, l_ref, acc_ref, None)
            return carry

        lax.fori_loop(0, i, body, 0)
        start = pl.multiple_of(i * tq, tq)
        _flash_step(q, k_ref[0, h, pl.ds(start, tq), :], v_ref[0, h, pl.ds(start, tq), :],
                    m_ref, l_ref, acc_ref, causal)
        o_ref[0, :, h * MLA_V_DIM:(h + 1) * MLA_V_DIM] = (acc_ref[...] / l_ref[...]).astype(o_ref.dtype)


def _mla_attn(q, k, v):
    b, nh, s, _ = q.shape
    tq = min(TQ_MLA, s)
    return pl.pallas_call(
        functools.partial(_mla_attn_kernel, tq=tq),
        out_shape=jax.ShapeDtypeStruct((b, s, nh * MLA_V_DIM), BF16),
        grid=(b, nh // 2, s // tq),
        in_specs=[pl.BlockSpec((1, 2, tq, LANES), lambda bi, hp, i: (bi, hp, i, 0)),
                  pl.BlockSpec((1, 2, s, LANES), lambda bi, hp, i: (bi, hp, 0, 0)),
                  pl.BlockSpec((1, 2, s, MLA_V_DIM), lambda bi, hp, i: (bi, hp, 0, 0))],
        out_specs=pl.BlockSpec((1, tq, 2 * MLA_V_DIM), lambda bi, hp, i: (bi, i, hp)),
        scratch_shapes=[pltpu.VMEM((tq, 1), F32), pltpu.VMEM((tq, 1), F32), pltpu.VMEM((tq, MLA_V_DIM), F32)],
        compiler_params=_params("parallel", "parallel", "arbitrary"),
        name="mla_attn",
    )(q, k, v)


def _conv_kernel(u_ref, halo_ref, bglu_ref, wdw_ref, bdw_ref, gln_ref, bln_ref, o_ref, ext_ref, *, ts):
    i = pl.program_id(1)
    bglu = bglu_ref[...]

    def glu(u2):
        u2 = u2 + bglu
        return u2[:, :CONV_CH] * jax.nn.sigmoid(u2[:, CONV_CH:])

    ext_ref[0:CONV_HALO, :] = jnp.where(i > 0, glu(halo_ref[0]), 0.0)
    ext_ref[CONV_HALO:, :] = glu(u_ref[0])
    wdw = wdw_ref[...]
    shift = CONV_HALO - (CONV_WIDTH - 1)
    for r0 in range(0, ts, CONV_CHUNK):
        acc = jnp.zeros((CONV_CHUNK, CONV_CH), F32)
        for k in range(CONV_WIDTH):
            acc = acc + wdw[k:k + 1, :] * ext_ref[r0 + shift + k:r0 + shift + k + CONV_CHUNK, :]
        acc = acc + bdw_ref[...]
        mu = jnp.mean(acc, axis=-1, keepdims=True)
        d = acc - mu
        var = jnp.mean(d * d, axis=-1, keepdims=True)
        y = d * lax.rsqrt(var + NORM_EPS) * gln_ref[...] + bln_ref[...]
        o_ref[0, r0:r0 + CONV_CHUNK, :] = (y * jax.nn.sigmoid(y)).astype(o_ref.dtype)


def _conv(u2, bglu, wdw, bdw, gln, bln, b, s):
    ts = min(TS_CONV, s)
    u3 = u2.reshape(b, s, 2 * CONV_CH)
    hb = ts // CONV_HALO
    full = lambda a: pl.BlockSpec(a.shape, lambda bi, i: (0,) * a.ndim)
    return pl.pallas_call(
        functools.partial(_conv_kernel, ts=ts),
        out_shape=jax.ShapeDtypeStruct((b, s, CONV_CH), BF16),
        grid=(b, s // ts),
        in_specs=[pl.BlockSpec((1, ts, 2 * CONV_CH), lambda bi, i: (bi, i, 0)),
                  pl.BlockSpec((1, CONV_HALO, 2 * CONV_CH), lambda bi, i: (bi, jnp.maximum(i * hb - 1, 0), 0)),
                  full(bglu), full(wdw), full(bdw), full(gln), full(bln)],
        out_specs=pl.BlockSpec((1, ts, CONV_CH), lambda bi, i: (bi, i, 0)),
        scratch_shapes=[pltpu.VMEM((ts + CONV_HALO, CONV_CH), F32)],
        compiler_params=_params("parallel", "parallel"),
        name="conv_mixer",
    )(u3, u3, bglu, wdw, bdw, gln, bln)


def _pair_norm_rope(x, g, cos, sin):
    lane = lax.broadcasted_iota(jnp.int32, x.shape, 1)
    lo = lane < NSA_HEAD_DIM
    x2 = x * x
    s_lo = jnp.sum(jnp.where(lo, x2, 0.0), axis=-1, keepdims=True)
    s_hi = jnp.sum(jnp.where(lo, 0.0, x2), axis=-1, keepdims=True)
    ss = jnp.where(lo, s_lo, s_hi) * (1.0 / NSA_HEAD_DIM)
    y = x * lax.rsqrt(ss + NORM_EPS) * g
    half = NSA_ROPE_DIM // 2
    first = (lane & (NSA_HEAD_DIM - 1)) < half
    rot = jnp.where(first, pltpu.roll(y, LANES - half, 1), pltpu.roll(y, half, 1))
    return y * cos + rot * sin


def _nsa_prep_kernel(qn_ref, ks_ref, vs_ref, kw_ref, vw_ref, cos_ref, sin_ref, gq_ref, gk_ref,
                     q_ref, kslc_ref, vslc_ref, kwin_ref, vwin_ref, *, tm):
    i = pl.program_id(1)
    cos = cos_ref[...]
    sin = sin_ref[...]
    dh = NSA_HEAD_DIM
    for p in range(NSA_HEADS // 2):
        y = _pair_norm_rope(qn_ref[:, p * LANES:(p + 1) * LANES], gq_ref[...], cos, sin).astype(q_ref.dtype)
        q_ref[0, 2 * p] = y[:, :dh]
        q_ref[0, 2 * p + 1] = y[:, dh:]
    ks = _pair_norm_rope(ks_ref[...], gk_ref[...], cos, sin).astype(kslc_ref.dtype)
    kw = _pair_norm_rope(kw_ref[...], gk_ref[...], cos, sin).astype(kwin_ref.dtype)
    vs = vs_ref[...].astype(vslc_ref.dtype)
    vw = vw_ref[...].astype(vwin_ref.dtype)
    blk = (i * tm + lax.broadcasted_iota(jnp.int32, (tm, dh), 0)) // SLC_BLOCK
    onehot = jnp.where(lax.broadcasted_iota(jnp.int32, (tm, dh), 1) == blk, 1.0, 0.0).astype(kslc_ref.dtype)
    for g in range(NSA_KV_GROUPS):
        kslc_ref[0, g, :, 0:dh] = ks[:, g * dh:(g + 1) * dh]
        kslc_ref[0, g, :, dh:2 * dh] = onehot
        kwin_ref[0, g] = kw[:, g * dh:(g + 1) * dh]
        vslc_ref[0, g] = vs[:, g * dh:(g + 1) * dh]
        vwin_ref[0, g] = vw[:, g * dh:(g + 1) * dh]


def _nsa_prep(qn, ks, vs, kw, vw, cos, sin, gq, gk, b, s):
    tm = min(TM_PREP, s)
    nb = s // tm
    row = lambda width: pl.BlockSpec((tm, width), lambda bi, i: (bi * nb + i, 0))
    tab = pl.BlockSpec((tm, LANES), lambda bi, i: (i, 0))
    full = lambda a: pl.BlockSpec(a.shape, lambda bi, i: (0,) * a.ndim)
    hd = lambda n, width: pl.BlockSpec((1, n, tm, width), lambda bi, i: (bi, 0, i, 0))
    g, dh = NSA_KV_GROUPS, NSA_HEAD_DIM
    return pl.pallas_call(
        functools.partial(_nsa_prep_kernel, tm=tm),
        out_shape=[jax.ShapeDtypeStruct((b, NSA_HEADS, s, dh), BF16),
                   jax.ShapeDtypeStruct((b, g, s, 2 * dh), BF16),
                   jax.ShapeDtypeStruct((b, g, s, dh), BF16),
                   jax.ShapeDtypeStruct((b, g, s, dh), BF16),
                   jax.ShapeDtypeStruct((b, g, s, dh), BF16)],
        grid=(b, nb),
        in_specs=[row(qn.shape[1]), row(LANES), row(LANES), row(LANES), row(LANES), tab, tab, full(gq), full(gk)],
        out_specs=[hd(NSA_HEADS, dh), hd(g, 2 * dh), hd(g, dh), hd(g, dh), hd(g, dh)],
        compiler_params=_params("parallel", "parallel"),
        name="nsa_prep",
    )(qn, ks, vs, kw, vw, cos, sin, gq, gk)


def _compress_kernel(kc_ref, vc_ref, pek_ref, pev_ref, wka_ref, wkb_ref, wk2_ref, wva_ref, wvb_ref, wv2_ref,
                     gk_ref, cos_ref, sin_ref, kcmp_ref, vcmp_ref):
    def comp(r_ref, pe_ref, wa_ref, wb_ref, w2_ref):
        r = r_ref[0]
        nr = r.shape[0]
        a = _dot((r + pe_ref[0:1, :]).astype(BF16), wa_ref[...])
        bb = _dot((r + pe_ref[1:2, :]).astype(BF16), wb_ref[...])
        hd = a + pltpu.roll(bb, nr - 1, 0)
        hd = hd * jax.nn.sigmoid(hd)
        return _dot(hd.astype(BF16), w2_ref[...])

    k = _pair_norm_rope(comp(kc_ref, pek_ref, wka_ref, wkb_ref, wk2_ref), gk_ref[...], cos_ref[...], sin_ref[...])
    v = comp(vc_ref, pev_ref, wva_ref, wvb_ref, wv2_ref)
    dh = NSA_HEAD_DIM
    for g in range(NSA_KV_GROUPS):
        kcmp_ref[0, g] = k[:, g * dh:(g + 1) * dh].astype(kcmp_ref.dtype)
        vcmp_ref[0, g] = v[:, g * dh:(g + 1) * dh].astype(vcmp_ref.dtype)


def _compress(kc, vc, pek, pev, wka, wkb, wk2, wva, wvb, wv2, gk, cos, sin, b, s):
    nr = s // CMP_STRIDE
    width = CMP_STRIDE * LANES
    kc3 = kc.reshape(b, nr, width)
    vc3 = vc.reshape(b, nr, width)
    full = lambda a: pl.BlockSpec(a.shape, lambda bi: (0,) * a.ndim)
    blk = pl.BlockSpec((1, nr, width), lambda bi: (bi, 0, 0))
    g, dh = NSA_KV_GROUPS, NSA_HEAD_DIM
    out = pl.BlockSpec((1, g, nr, dh), lambda bi: (bi, 0, 0, 0))
    return pl.pallas_call(
        _compress_kernel,
        out_shape=[jax.ShapeDtypeStruct((b, g, nr, dh), BF16)] * 2,
        grid=(b,),
        in_specs=[blk, blk, full(pek), full(pev), full(wka), full(wkb), full(wk2), full(wva), full(wvb), full(wv2),
                  full(gk), full(cos), full(sin)],
        out_specs=[out, out],
        compiler_params=_params("parallel"),
        name="nsa_compress",
    )(kc3, vc3, pek, pev, wka, wkb, wk2, wva, wvb, wv2, gk, cos, sin)


def _nsa_attn_kernel(q_ref, kc_ref, vc_ref, ks_ref, vs_ref, kw_ref, vw_ref, gate_ref, ovl_ref, o_ref,
                     qa_ref, m_ref, l_ref, acc_ref, *, tq, n_slc):
    i = pl.program_id(2)
    q0 = i * tq
    hpg, dh = NSA_HPG, NSA_HEAD_DIM
    rows = hpg * tq
    q4 = jnp.concatenate([q_ref[0, h] for h in range(hpg)], axis=0)

    kc = kc_ref[0, 0]
    nc = kc.shape[0]
    s = _dot_nt(q4, kc)
    r = lax.broadcasted_iota(jnp.int32, (rows, nc), 0) & (tq - 1)
    c = lax.broadcasted_iota(jnp.int32, (rows, nc), 1)
    mask = c * CMP_STRIDE + (CMP_BLOCK - 1) <= q0 + r
    s = jnp.where(mask, s, MASK_VALUE)
    p = jnp.where(mask, jnp.exp(s - jnp.max(s, axis=-1, keepdims=True)), 0.0)
    p = p / jnp.maximum(jnp.sum(p, axis=-1, keepdims=True), 1e-30)
    o_cmp = _dot(p.astype(BF16), vc_ref[0, 0])

    ps = p[0:tq]
    for h in range(1, hpg):
        ps = ps + p[h * tq:(h + 1) * tq]
    hi = ps.astype(BF16)
    rem = ps - hi.astype(F32)
    mid = rem.astype(BF16)
    low = (rem - mid.astype(F32)).astype(BF16)
    ovl = ovl_ref[...]
    imp = _dot_nt(ovl, hi) + _dot_nt(ovl, mid) + _dot_nt(ovl, low)
    jrow = lax.broadcasted_iota(jnp.int32, (dh, tq), 0)
    blk_t = (q0 + lax.broadcasted_iota(jnp.int32, (dh, tq), 1)) // SLC_BLOCK
    forced = (jrow == 0) | (jrow == blk_t) | (jrow == blk_t - 1)
    imp = jnp.where(forced, FORCED_BLOCK_SCORE, imp)
    imp = jnp.where(jrow <= blk_t, imp, -1.0)
    n_slab = -(-n_slc // SUBLANES)
    slabs = [imp[SUBLANES * c:SUBLANES * (c + 1), :] for c in range(n_slab)]
    ranks = [jnp.zeros((SUBLANES, tq), jnp.int32) for _ in range(n_slab)]
    sub = lax.broadcasted_iota(jnp.int32, (SUBLANES, tq), 0)
    for jp in range(n_slc):
        other = imp[jp:jp + 1, :]
        for c in range(n_slab):
            if SUBLANES * c > jp:
                ranks[c] = ranks[c] + jnp.where(other >= slabs[c], 1, 0)
            elif SUBLANES * (c + 1) - 1 <= jp:
                ranks[c] = ranks[c] + jnp.where(other > slabs[c], 1, 0)
            else:
                ge = jnp.where(other >= slabs[c], 1, 0)
                gt = jnp.where(other > slabs[c], 1, 0)
                ranks[c] = ranks[c] + jnp.where(sub > jp - SUBLANES * c, ge, gt)
    rank = jnp.concatenate(ranks + [jnp.full((LANES - SUBLANES * n_slab, tq), SLC_TOP_N, jnp.int32)], axis=0)
    bias_t = jnp.where(rank < SLC_TOP_N, 0.0, MASK_VALUE)
    bias = bias_t.T[:, 0:dh].astype(BF16)
    for h in range(hpg):
        qa_ref[h * tq:(h + 1) * tq, 0:dh] = q_ref[0, h]
        qa_ref[h * tq:(h + 1) * tq, dh:2 * dh] = bias
    qa = qa_ref[...]

    rr = lax.broadcasted_iota(jnp.int32, (rows, tq), 0) & (tq - 1)
    cc = lax.broadcasted_iota(jnp.int32, (rows, tq), 1)
    causal = cc <= rr

    _flash_init(m_ref, l_ref, acc_ref)

    def slc_body(j, carry):
        start = pl.multiple_of(j * tq, tq)
        _flash_step(qa, ks_ref[0, 0, pl.ds(start, tq), :], vs_ref[0, 0, pl.ds(start, tq), :],
                    m_ref, l_ref, acc_ref, None)
        return carry

    lax.fori_loop(0, i, slc_body, 0)
    start = pl.multiple_of(q0, tq)
    _flash_step(qa, ks_ref[0, 0, pl.ds(start, tq), :], vs_ref[0, 0, pl.ds(start, tq), :],
                m_ref, l_ref, acc_ref, causal)
    o_slc = acc_ref[...] / l_ref[...]

    _flash_init(m_ref, l_ref, acc_ref)
    n_back = WINDOW // tq
    for d in range(n_back, 0, -1):
        @pl.when(i >= d)
        def _(d=d):
            st = pl.multiple_of((i - d) * tq, tq)
            _flash_step(q4, kw_ref[0, 0, pl.ds(st, tq), :], vw_ref[0, 0, pl.ds(st, tq), :],
                        m_ref, l_ref, acc_ref, (cc > rr) if d == n_back else None)
    _flash_step(q4, kw_ref[0, 0, pl.ds(start, tq), :], vw_ref[0, 0, pl.ds(start, tq), :],
                m_ref, l_ref, acc_ref, causal)
    o_win = acc_ref[...] / l_ref[...]

    sg = jax.nn.sigmoid(gate_ref[0])
    for h in range(hpg):
        sl = slice(h * tq, (h + 1) * tq)
        o = (sg[:, 3 * h:3 * h + 1] * o_cmp[sl] + sg[:, 3 * h + 1:3 * h + 2] * o_slc[sl]
             + sg[:, 3 * h + 2:3 * h + 3] * o_win[sl])
        o_ref[0, :, h * dh:(h + 1) * dh] = o.astype(o_ref.dtype)


def _nsa_attn(q, kcmp, vcmp, kslc, vslc, kwin, vwin, gates, ovl, b, s):
    tq = min(TQ_NSA, s)
    assert WINDOW % tq == 0
    n_slc = s // SLC_BLOCK
    assert n_slc <= NSA_HEAD_DIM
    g, dh, hpg = NSA_KV_GROUPS, NSA_HEAD_DIM, NSA_HPG
    nc = kcmp.shape[2]
    gates3 = gates.reshape(b, s, g * LANES)
    seq = lambda width: pl.BlockSpec((1, 1, s, width), lambda bi, gi, i: (bi, gi, 0, 0))
    cmp_spec = pl.BlockSpec((1, 1, nc, dh), lambda bi, gi, i: (bi, gi, 0, 0))
    return pl.pallas_call(
        functools.partial(_nsa_attn_kernel, tq=tq, n_slc=n_slc),
        out_shape=jax.ShapeDtypeStruct((b, s, NSA_HEADS * dh), BF16),
        grid=(b, g, s // tq),
        in_specs=[pl.BlockSpec((1, hpg, tq, dh), lambda bi, gi, i: (bi, gi, i, 0)),
                  cmp_spec, cmp_spec, seq(2 * dh), seq(dh), seq(dh), seq(dh),
                  pl.BlockSpec((1, tq, LANES), lambda bi, gi, i: (bi, i, gi)),
                  pl.BlockSpec(ovl.shape, lambda bi, gi, i: (0, 0))],
        out_specs=pl.BlockSpec((1, tq, hpg * dh), lambda bi, gi, i: (bi, i, gi)),
        scratch_shapes=[pltpu.VMEM((hpg * tq, 2 * dh), BF16), pltpu.VMEM((hpg * tq, 1), F32),
                        pltpu.VMEM((hpg * tq, 1), F32), pltpu.VMEM((hpg * tq, dh), F32)],
        compiler_params=_params("parallel", "parallel", "arbitrary"),
        name="nsa_attn",
    )(q, kcmp, vcmp, kslc, vslc, kwin, vwin, gates3, ovl)


def _merge_kernel(x_ref, oa_ref, cv_ref, on_ref, gm_ref, woa_ref, wcv_ref, bcv_ref, won_ref, wout_ref, o_ref):
    d = x_ref.shape[1]
    o_a = _dot(oa_ref[...], woa_ref[...])
    o_b = _dot(cv_ref[...], wcv_ref[...]) + bcv_ref[...]
    o_c = _dot(on_ref[...], won_ref[...])
    mix = (jax.nn.sigmoid(gm_ref[:, 0:d]) * o_a + jax.nn.sigmoid(gm_ref[:, d:2 * d]) * o_b
           + jax.nn.sigmoid(gm_ref[:, 2 * d:3 * d]) * o_c)
    o_ref[...] = x_ref[...] + _dot(mix.astype(BF16), wout_ref[...])


def _merge(x2d, oa, cv, on, gm, woa, wcv, bcv, won, wout):
    t, d = x2d.shape
    tm = min(TM_MERGE, t)
    row = lambda a: pl.BlockSpec((tm, a.shape[1]), lambda i: (i, 0))
    full = lambda a: pl.BlockSpec(a.shape, lambda i: (0,) * a.ndim)
    return pl.pallas_call(
        _merge_kernel,
        out_shape=jax.ShapeDtypeStruct((t, d), F32),
        grid=(t // tm,),
        in_specs=[row(x2d), row(oa), row(cv), row(on), row(gm), full(woa), full(wcv), full(bcv), full(won), full(wout)],
        out_specs=pl.BlockSpec((tm, d), lambda i: (i, 0)),
        compiler_params=_params("parallel"),
        name="merge",
    )(x2d, oa, cv, on, gm, woa, wcv, bcv, won, wout)


def _ffn_kernel(x_ref, g_ref, w1_ref, w2_ref, o_ref, h_ref, acc_ref):
    j = pl.program_id(1)

    @pl.when(j == 0)
    def _():
        x = x_ref[...]
        h_ref[...] = _rms(x, g_ref[...]).astype(h_ref.dtype)
        acc_ref[...] = x

    a = jnp.maximum(_dot(h_ref[...], w1_ref[...]), 0.0)
    acc_ref[...] += _dot((a * a).astype(BF16), w2_ref[...])

    @pl.when(j == pl.num_programs(1) - 1)
    def _():
        o_ref[...] = acc_ref[...]


def _ffn(x2d, g, w1, w2):
    t, d = x2d.shape
    f = w1.shape[1]
    tm = min(TM_FFN, t)
    tf = min(TF_FFN, f)
    return pl.pallas_call(
        _ffn_kernel,
        out_shape=jax.ShapeDtypeStruct((t, d), F32),
        grid=(t // tm, f // tf),
        in_specs=[pl.BlockSpec((tm, d), lambda i, j: (i, 0)),
                  pl.BlockSpec((1, d), lambda i, j: (0, 0)),
                  pl.BlockSpec((d, tf), lambda i, j: (0, j)),
                  pl.BlockSpec((tf, d), lambda i, j: (j, 0))],
        out_specs=pl.BlockSpec((tm, d), lambda i, j: (i, 0)),
        scratch_shapes=[pltpu.VMEM((tm, d), BF16), pltpu.VMEM((tm, d), F32)],
        compiler_params=_params("parallel", "arbitrary"),
        name="ffn",
    )(x2d, g, w1, w2)


def _rope_tables(pos, rot_dim, group, width=LANES):
    half = rot_dim // 2
    inv = jnp.power(jnp.float32(ROPE_THETA), -jnp.arange(half, dtype=jnp.float32) * (2.0 / rot_dim))
    ang = pos.astype(jnp.float32)[:, None] * inv
    n = ang.shape[0]
    cos = jnp.concatenate([jnp.cos(ang), jnp.cos(ang), jnp.ones((n, group - rot_dim), F32)], axis=1)
    sin = jnp.concatenate([-jnp.sin(ang), jnp.sin(ang), jnp.zeros((n, group - rot_dim), F32)], axis=1)
    reps = width // group
    return jnp.tile(cos, (1, reps)), jnp.tile(sin, (1, reps))


def _pad_cols(w, width):
    return jnp.pad(w, ((0, 0), (0, width - w.shape[1])))


def _overlap_t(s):
    n_cmp_pad = s // CMP_STRIDE
    n_slc = s // SLC_BLOCK
    c_start = np.arange(n_cmp_pad) * CMP_STRIDE
    s_start = np.arange(n_slc) * SLC_BLOCK
    ov = ((c_start[None, :] < s_start[:, None] + SLC_BLOCK) & (c_start[None, :] + CMP_BLOCK > s_start[:, None]))
    out = np.zeros((NSA_HEAD_DIM, n_cmp_pad), np.float32)
    out[:n_slc] = ov
    return jnp.asarray(out, BF16)


def _compress_weights(w1, w2):
    g, dh, hid = NSA_KV_GROUPS, NSA_HEAD_DIM, CMP_HIDDEN
    w1r = w1.reshape(2, CMP_STRIDE, dh, hid)
    wide = jnp.zeros((2, CMP_STRIDE, g, dh, g, hid), w1.dtype)
    for gi in range(g):
        wide = wide.at[:, :, gi, :, gi, :].set(w1r)
    wide = wide.reshape(2, CMP_STRIDE * g * dh, g * hid).astype(BF16)
    w2w = jnp.zeros((g, hid, g, dh), w2.dtype)
    for gi in range(g):
        w2w = w2w.at[gi, :, gi, :].set(w2)
    return wide[0], wide[1], w2w.reshape(g * hid, g * dh).astype(BF16)


def _compress_pe(pe):
    pe2 = pe.reshape(2, CMP_STRIDE, 1, NSA_HEAD_DIM)
    return jnp.broadcast_to(pe2, (2, CMP_STRIDE, NSA_KV_GROUPS, NSA_HEAD_DIM)).reshape(2, -1).astype(F32)


def kernel(x, g_mix, w_in, g_cq, g_ckv, w_uq, w_ukv, g_q_mla, g_k_mla, w_o_mla, b_glu, w_dw, b_dw, g_conv_ln, b_conv_ln, w_conv_out, b_conv_out, pe_cmp_k, pe_cmp_v, w_cmp_k1, w_cmp_k2, w_cmp_v1, w_cmp_v2, g_q_nsa, g_k_nsa, w_o_nsa, w_out, g_ffn, w_ff1, w_ff2):
    b, s, d = x.shape
    depth = w_in.shape[0]
    t = b * s
    q_rank, kv_rank = g_cq.shape[1], g_ckv.shape[1]
    nh, hd = NSA_HEADS, NSA_HEAD_DIM
    ngd = NSA_KV_GROUPS * hd
    assert ngd == LANES and s % CMP_STRIDE == 0

    pos = jnp.arange(s)
    cos_mla, sin_mla = _rope_tables(pos, MLA_ROPE_DIM, LANES)
    cos_nsa, sin_nsa = _rope_tables(pos, NSA_ROPE_DIM, hd)
    cmp_pos = jnp.arange(s // CMP_STRIDE) * CMP_STRIDE + CMP_BLOCK - 1
    cos_cmp, sin_cmp = _rope_tables(cmp_pos, NSA_ROPE_DIM, hd)
    ovl = _overlap_t(s)

    splits = (q_rank, kv_rank, MLA_ROPE_DIM, 2 * CONV_CH, nh * hd, 6 * ngd, 3 * nh, 3 * d)
    offs = np.concatenate([[0], np.cumsum(splits)])
    widths = (q_rank, kv_rank, LANES, 2 * CONV_CH, nh * hd) + (LANES,) * 6 + (NSA_KV_GROUPS * LANES, 3 * d)

    x2d = x.reshape(t, d)
    for l in range(depth):
        wl = w_in[l]
        seg = [wl[:, offs[k]:offs[k + 1]] for k in range(len(splits))]
        gate_cols = [_pad_cols(seg[6][:, gi * NSA_HPG * 3:(gi + 1) * NSA_HPG * 3], LANES) for gi in range(NSA_KV_GROUPS)]
        w_in_p = jnp.concatenate(
            [seg[0], seg[1], _pad_cols(seg[2], LANES), seg[3], seg[4], seg[5]] + gate_cols + [seg[7]], axis=1).astype(BF16)
        (cq, ckv, kr, u2, qn, kc, vc, ks, vs, kw, vw, gn, gm) = _inproj(x2d, g_mix[l][None, :], w_in_p, widths)

        wuq = w_uq[l].reshape(q_rank, MLA_HEADS, MLA_QK_DIM)
        wuq = jnp.pad(wuq, ((0, 0), (0, 0), (0, LANES - MLA_QK_DIM))).reshape(q_rank, MLA_HEADS * LANES).astype(BF16)
        wukv = w_ukv[l].reshape(kv_rank, MLA_HEADS, MLA_NOPE_DIM + MLA_V_DIM)
        wuk = jnp.pad(wukv[:, :, :MLA_NOPE_DIM], ((0, 0), (0, 0), (MLA_ROPE_DIM, LANES - MLA_QK_DIM)))
        wuk = wuk.reshape(kv_rank, MLA_HEADS * LANES).astype(BF16)
        wuv = wukv[:, :, MLA_NOPE_DIM:].reshape(kv_rank, MLA_HEADS * MLA_V_DIM).astype(BF16)
        gq = _pad_cols((g_q_mla[l] * (MLA_QK_DIM ** -0.5))[None, :], LANES)
        gk = _pad_cols(g_k_mla[l][None, :], LANES)
        q_m, k_m, v_m = _mla_prep(cq, ckv, kr, cos_mla, sin_mla, g_cq[l][None, :], g_ckv[l][None, :],
                                  wuq, wuk, wuv, gq, gk, b, s)
        o_mla = _mla_attn(q_m, k_m, v_m).reshape(t, MLA_HEADS * MLA_V_DIM)

        wdw = jnp.pad(w_dw[l], ((0, CONV_HALO - CONV_WIDTH), (0, 0)))
        o_cv = _conv(u2, b_glu[l][None, :], wdw, b_dw[l][None, :], g_conv_ln[l][None, :], b_conv_ln[l][None, :],
                     b, s).reshape(t, CONV_CH)

        gqn = jnp.tile(g_q_nsa[l] * (hd ** -0.5), 2)[None, :]
        gkn = jnp.tile(g_k_nsa[l], 2)[None, :]
        q_n, k_slc, v_slc, k_win, v_win = _nsa_prep(qn, ks, vs, kw, vw, cos_nsa, sin_nsa, gqn, gkn, b, s)
        wka, wkb, wk2 = _compress_weights(w_cmp_k1[l], w_cmp_k2[l])
        wva, wvb, wv2 = _compress_weights(w_cmp_v1[l], w_cmp_v2[l])
        k_cmp, v_cmp = _compress(kc, vc, _compress_pe(pe_cmp_k[l]), _compress_pe(pe_cmp_v[l]),
                                 wka, wkb, wk2, wva, wvb, wv2, gkn, cos_cmp, sin_cmp, b, s)
        o_nsa = _nsa_attn(q_n, k_cmp, v_cmp, k_slc, v_slc, k_win, v_win, gn, ovl, b, s).reshape(t, nh * hd)

        x2d = _merge(x2d, o_mla, o_cv, o_nsa, gm, w_o_mla[l].astype(BF16), w_conv_out[l].astype(BF16),
                     b_conv_out[l][None, :], w_o_nsa[l].astype(BF16), w_out[l].astype(BF16))
        x2d = _ffn(x2d, g_ffn[l][None, :], w_ff1[l].astype(BF16), w_ff2[l].astype(BF16))
    return x2d.reshape(b, s, d)
```

```python
import functools
import math

import numpy as np
import jax
import jax.numpy as jnp
from jax import lax
from jax.experimental import pallas as pl
from jax.experimental.pallas import tpu as pltpu

F32 = jnp.float32
BF16 = jnp.bfloat16

ROPE_THETA = 500000.0
NORM_EPS = 1e-6
MLA_HEADS = 8
MLA_NOPE_DIM = 64
MLA_ROPE_DIM = 32
MLA_QK_DIM = MLA_NOPE_DIM + MLA_ROPE_DIM
CONV_CH = 512
CONV_WIDTH = 31
NSA_HEADS = 8
NSA_KV_GROUPS = 2
NSA_HEAD_DIM = 64
NSA_ROPE_DIM = NSA_HEAD_DIM // 4
NSA_HPG = NSA_HEADS // NSA_KV_GROUPS
CMP_BLOCK = 32
CMP_STRIDE = 16
CMP_HIDDEN = 128
SLC_BLOCK = 64
SLC_TOP_N = 16
WINDOW = 512
FORCED_BLOCK_SCORE = 1e6
V_DIM = 64

LANES = 128
SUBLANES = 8
BF16_SUBLANES = 16
VMEM_LIMIT_BYTES = 56 * 1024 * 1024

MASK_VALUE = -1e30
LOG2_E = math.log2(math.e)
V_AUG = V_DIM + BF16_SUBLANES

TM_PROJ = 256
TM_PREP = 512
TQ_ATTN = 256
KEY_CHUNK = 256
TS_CONV = 256
CONV_HALO = 32
CONV_CHUNK = 64
TM_MERGE = 512
TM_FFN = 1024
TF_FFN = 512


def _dot(a, b):
    return jnp.dot(a, b, preferred_element_type=F32)


def _rms(x, g):
    return x * lax.rsqrt(jnp.mean(x * x, axis=-1, keepdims=True) + NORM_EPS) * g


def _params(*sem):
    return pltpu.CompilerParams(dimension_semantics=sem, vmem_limit_bytes=VMEM_LIMIT_BYTES)


def _flash_scratch(n_chains, n):
    return [pltpu.VMEM((n_chains, 1, n), F32), pltpu.VMEM((n_chains, V_AUG, n), F32)]


def _flash_reset(st):
    m_ref, acc_ref = st
    m_ref[...] = jnp.full(m_ref.shape, MASK_VALUE, F32)
    acc_ref[...] = jnp.zeros(acc_ref.shape, F32)


def _flash_steps(st, ops):
    m_ref, acc_ref = st
    scores = [_dot(k, qt) for (_, qt, k, _, _) in ops]
    probs = []
    for (n, _, _, _, mask), s in zip(ops, scores):
        if mask is not None:
            s = jnp.where(mask, s, MASK_VALUE)
        m = m_ref[n]
        m_new = jnp.maximum(m, jnp.max(s, axis=0, keepdims=True))
        m_ref[n] = m_new
        probs.append((jnp.exp2(m - m_new), jnp.exp2(s - m_new).astype(BF16)))
    pvs = [_dot(vt, p) for (_, _, _, vt, _), (_, p) in zip(ops, probs)]
    for (n, _, _, _, _), (alpha, _), pv in zip(ops, probs, pvs):
        acc_ref[n] = alpha * acc_ref[n] + pv


def _flash_result(st, n):
    _, acc_ref = st
    return acc_ref[n, 0:V_DIM, :] / acc_ref[n, V_DIM:V_DIM + 1, :]


def _causal_sweep(st, chains, i, tq):
    _flash_reset(st)

    def ops_for(c, mask):
        return [(n, qt_fn(), k_fn(c), vt_fn(c), mask) for n, (qt_fn, k_fn, vt_fn) in enumerate(chains)]

    def pair(j, carry):
        _flash_steps(st, ops_for(2 * j, None) + ops_for(2 * j + 1, None))
        return carry

    lax.fori_loop(0, i // 2, pair, 0)

    @pl.when(i % 2 == 1)
    def _():
        _flash_steps(st, ops_for(i - 1, None))

    key = lax.broadcasted_iota(jnp.int32, (KEY_CHUNK, tq), 0)
    qry = lax.broadcasted_iota(jnp.int32, (KEY_CHUNK, tq), 1)
    _flash_steps(st, ops_for(i, key <= qry))


def _ones_rows(n):
    return jnp.where(lax.broadcasted_iota(jnp.int32, (V_AUG - V_DIM, n), 0) == 0, 1.0, 0.0).astype(BF16)


def _store_vt(v, vt_ref, n_heads):
    ones = _ones_rows(KEY_CHUNK)
    for p in range(n_heads // 2):
        t = v[:, p * LANES:(p + 1) * LANES].T.astype(vt_ref.dtype)
        for c in range(vt_ref.shape[2]):
            for e in range(2):
                vt_ref[0, 2 * p + e, c, 0:V_DIM, :] = t[e * V_DIM:(e + 1) * V_DIM, c * KEY_CHUNK:(c + 1) * KEY_CHUNK]
                vt_ref[0, 2 * p + e, c, V_DIM:V_AUG, :] = ones


def _vt_shape(b, n, s):
    return jax.ShapeDtypeStruct((b, n, s // KEY_CHUNK, V_AUG, KEY_CHUNK), BF16)


def _inproj_kernel(x_ref, g_ref, w_ref, *out_refs, widths):
    h = _rms(x_ref[...], g_ref[...]).astype(BF16)
    off = 0
    for o_ref, wd in zip(out_refs, widths):
        o_ref[...] = _dot(h, w_ref[:, off:off + wd]).astype(o_ref.dtype)
        off += wd


def _inproj(x2d, g, w, widths):
    t, d = x2d.shape
    n = w.shape[1]
    return pl.pallas_call(
        functools.partial(_inproj_kernel, widths=widths),
        out_shape=[jax.ShapeDtypeStruct((t, wd), F32) for wd in widths],
        grid=(t // TM_PROJ,),
        in_specs=[pl.BlockSpec((TM_PROJ, d), lambda i: (i, 0)),
                  pl.BlockSpec((1, d), lambda i: (0, 0)),
                  pl.BlockSpec((d, n), lambda i: (0, 0), pipeline_mode=pl.Buffered(1))],
        out_specs=[pl.BlockSpec((TM_PROJ, wd), lambda i: (i, 0)) for wd in widths],
        compiler_params=_params("parallel"),
        name="inproj",
    )(x2d, g, w)


def _mla_prep_kernel(cq_ref, ckv_ref, kr_ref, cos_ref, sin_ref, gcq_ref, gckv_ref,
                     wuq_ref, wuk_ref, wuv_ref, gq_ref, gk_ref, qt_ref, k_ref, vt_ref):
    cos = cos_ref[...]
    sin = sin_ref[...]
    half = MLA_ROPE_DIM // 2
    first = lax.broadcasted_iota(jnp.int32, cos.shape, 1) < half

    def rope(x):
        rot = jnp.where(first, pltpu.roll(x, LANES - half, 1), pltpu.roll(x, half, 1))
        return x * cos + rot * sin

    inv_dim = 1.0 / MLA_QK_DIM
    q = _dot(_rms(cq_ref[...], gcq_ref[...]).astype(BF16), wuq_ref[...])
    gq = gq_ref[...]
    for h in range(MLA_HEADS):
        qh = q[:, h * LANES:(h + 1) * LANES]
        r = lax.rsqrt(jnp.sum(qh * qh, axis=-1, keepdims=True) * inv_dim + NORM_EPS)
        qt_ref[0, h] = rope(qh * r * gq).T.astype(qt_ref.dtype)

    ckvn = _rms(ckv_ref[...], gckv_ref[...]).astype(BF16)
    kn = _dot(ckvn, wuk_ref[...])
    v = _dot(ckvn, wuv_ref[...])
    kr = kr_ref[...]
    gk = gk_ref[...]
    for h in range(MLA_HEADS):
        kh = kn[:, h * LANES:(h + 1) * LANES] + kr
        r = lax.rsqrt(jnp.sum(kh * kh, axis=-1, keepdims=True) * inv_dim + NORM_EPS)
        k_ref[0, h] = rope(kh * r * gk).astype(k_ref.dtype)
    _store_vt(v, vt_ref, MLA_HEADS)


def _mla_prep(cq, ckv, kr, cos, sin, gcq, gckv, wuq, wuk, wuv, gq, gk, b, s):
    tm = min(TM_PREP, s)
    nb = s // tm
    row = lambda width: pl.BlockSpec((tm, width), lambda bi, i: (bi * nb + i, 0))
    tab = pl.BlockSpec((tm, LANES), lambda bi, i: (i, 0))
    full = lambda a: pl.BlockSpec(a.shape, lambda bi, i: (0,) * a.ndim)
    return pl.pallas_call(
        _mla_prep_kernel,
        out_shape=[jax.ShapeDtypeStruct((b, MLA_HEADS, LANES, s), BF16),
                   jax.ShapeDtypeStruct((b, MLA_HEADS, s, LANES), BF16),
                   _vt_shape(b, MLA_HEADS, s)],
        grid=(b, nb),
        in_specs=[row(cq.shape[1]), row(ckv.shape[1]), row(LANES), tab, tab,
                  full(gcq), full(gckv), full(wuq), full(wuk), full(wuv), full(gq), full(gk)],
        out_specs=[pl.BlockSpec((1, MLA_HEADS, LANES, tm), lambda bi, i: (bi, 0, 0, i)),
                   pl.BlockSpec((1, MLA_HEADS, tm, LANES), lambda bi, i: (bi, 0, i, 0)),
                   pl.BlockSpec((1, MLA_HEADS, tm // KEY_CHUNK, V_AUG, KEY_CHUNK), lambda bi, i: (bi, 0, i, 0, 0))],
        compiler_params=_params("parallel", "parallel"),
        name="mla_prep",
    )(cq, ckv, kr, cos, sin, gcq, gckv, wuq, wuk, wuv, gq, gk)


def _mla_attn_kernel(qt_ref, k_ref, vt_ref, o_ref, m_ref, acc_ref, *, tq):
    i = pl.program_id(1)
    st = (m_ref, acc_ref)
    chains = [(lambda h=h: qt_ref[0, h],
               lambda c, h=h: k_ref[0, h, pl.ds(pl.multiple_of(c * KEY_CHUNK, KEY_CHUNK), KEY_CHUNK), :],
               lambda c, h=h: vt_ref[0, h, c]) for h in range(MLA_HEADS)]
    _causal_sweep(st, chains, i, tq)
    outs = [_flash_result(st, h) for h in range(MLA_HEADS)]
    o_ref[0] = jnp.concatenate(outs, axis=0).T.astype(o_ref.dtype)


def _mla_attn(qt, k, vt):
    b, nh, s, _ = k.shape
    tq = TQ_ATTN
    assert tq == KEY_CHUNK and s % tq == 0
    return pl.pallas_call(
        functools.partial(_mla_attn_kernel, tq=tq),
        out_shape=jax.ShapeDtypeStruct((b, s, nh * V_DIM), BF16),
        grid=(b, s // tq),
        in_specs=[pl.BlockSpec((1, nh, LANES, tq), lambda bi, i: (bi, 0, 0, i)),
                  pl.BlockSpec((1, nh, s, LANES), lambda bi, i: (bi, 0, 0, 0)),
                  pl.BlockSpec((1, nh, s // KEY_CHUNK, V_AUG, KEY_CHUNK), lambda bi, i: (bi, 0, 0, 0, 0))],
        out_specs=pl.BlockSpec((1, tq, nh * V_DIM), lambda bi, i: (bi, i, 0)),
        scratch_shapes=_flash_scratch(nh, tq),
        compiler_params=_params("parallel", "arbitrary"),
        name="mla_attn",
    )(qt, k, vt)


def _conv_kernel(u_ref, halo_ref, bglu_ref, wdw_ref, bdw_ref, gln_ref, bln_ref, o_ref, ext_ref, *, ts):
    i = pl.program_id(1)
    bglu = bglu_ref[...]

    def glu(u2):
        u2 = u2 + bglu
        return u2[:, :CONV_CH] * jax.nn.sigmoid(u2[:, CONV_CH:])

    ext_ref[0:CONV_HALO, :] = jnp.where(i > 0, glu(halo_ref[0]), 0.0)
    ext_ref[CONV_HALO:, :] = glu(u_ref[0])
    wdw = wdw_ref[...]
    shift = CONV_HALO - (CONV_WIDTH - 1)
    for r0 in range(0, ts, CONV_CHUNK):
        acc = jnp.zeros((CONV_CHUNK, CONV_CH), F32)
        for k in range(CONV_WIDTH):
            acc = acc + wdw[k:k + 1, :] * ext_ref[r0 + shift + k:r0 + shift + k + CONV_CHUNK, :]
        acc = acc + bdw_ref[...]
        mu = jnp.mean(acc, axis=-1, keepdims=True)
        d = acc - mu
        var = jnp.mean(d * d, axis=-1, keepdims=True)
        y = d * lax.rsqrt(var + NORM_EPS) * gln_ref[...] + bln_ref[...]
        o_ref[0, r0:r0 + CONV_CHUNK, :] = (y * jax.nn.sigmoid(y)).astype(o_ref.dtype)


def _conv(u2, bglu, wdw, bdw, gln, bln, b, s):
    ts = min(TS_CONV, s)
    u3 = u2.reshape(b, s, 2 * CONV_CH)
    hb = ts // CONV_HALO
    full = lambda a: pl.BlockSpec(a.shape, lambda bi, i: (0,) * a.ndim)
    return pl.pallas_call(
        functools.partial(_conv_kernel, ts=ts),
        out_shape=jax.ShapeDtypeStruct((b, s, CONV_CH), BF16),
        grid=(b, s // ts),
        in_specs=[pl.BlockSpec((1, ts, 2 * CONV_CH), lambda bi, i: (bi, i, 0)),
                  pl.BlockSpec((1, CONV_HALO, 2 * CONV_CH), lambda bi, i: (bi, jnp.maximum(i * hb - 1, 0), 0)),
                  full(bglu), full(wdw), full(bdw), full(gln), full(bln)],
        out_specs=pl.BlockSpec((1, ts, CONV_CH), lambda bi, i: (bi, i, 0)),
        scratch_shapes=[pltpu.VMEM((ts + CONV_HALO, CONV_CH), F32)],
        compiler_params=_params("parallel", "parallel"),
        name="conv_mixer",
    )(u3, u3, bglu, wdw, bdw, gln, bln)


def _pair_norm_rope(x, g, cos, sin):
    lane = lax.broadcasted_iota(jnp.int32, x.shape, 1)
    lo = lane < NSA_HEAD_DIM
    x2 = x * x
    s_lo = jnp.sum(jnp.where(lo, x2, 0.0), axis=-1, keepdims=True)
    s_hi = jnp.sum(jnp.where(lo, 0.0, x2), axis=-1, keepdims=True)
    ss = jnp.where(lo, s_lo, s_hi) * (1.0 / NSA_HEAD_DIM)
    y = x * lax.rsqrt(ss + NORM_EPS) * g
    half = NSA_ROPE_DIM // 2
    first = (lane & (NSA_HEAD_DIM - 1)) < half
    rot = jnp.where(first, pltpu.roll(y, LANES - half, 1), pltpu.roll(y, half, 1))
    return y * cos + rot * sin


def _nsa_prep_kernel(qn_ref, ks_ref, vs_ref, kw_ref, vw_ref, gate_ref, cos_ref, sin_ref, gq_ref, gk_ref,
                     qt_ref, kslc_ref, vslct_ref, kwin_ref, vwint_ref, gatet_ref, *, tm):
    i = pl.program_id(1)
    cos = cos_ref[...]
    sin = sin_ref[...]
    dh = NSA_HEAD_DIM
    for p in range(NSA_HEADS // 2):
        yt = _pair_norm_rope(qn_ref[:, p * LANES:(p + 1) * LANES], gq_ref[...], cos, sin).T.astype(qt_ref.dtype)
        qt_ref[0, 2 * p] = yt[0:dh]
        qt_ref[0, 2 * p + 1] = yt[dh:2 * dh]
    ks = _pair_norm_rope(ks_ref[...], gk_ref[...], cos, sin).astype(kslc_ref.dtype)
    kw = _pair_norm_rope(kw_ref[...], gk_ref[...], cos, sin).astype(kwin_ref.dtype)
    blk = (i * tm + lax.broadcasted_iota(jnp.int32, (tm, dh), 0)) // SLC_BLOCK
    onehot = jnp.where(lax.broadcasted_iota(jnp.int32, (tm, dh), 1) == blk, 1.0, 0.0).astype(kslc_ref.dtype)
    for g in range(NSA_KV_GROUPS):
        kslc_ref[0, g, :, 0:dh] = ks[:, g * dh:(g + 1) * dh]
        kslc_ref[0, g, :, dh:2 * dh] = onehot
        kwin_ref[0, g] = kw[:, g * dh:(g + 1) * dh]
    _store_vt(vs_ref[...], vslct_ref, NSA_KV_GROUPS)
    _store_vt(vw_ref[...], vwint_ref, NSA_KV_GROUPS)
    for g in range(NSA_KV_GROUPS):
        gatet_ref[0, g] = gate_ref[:, g * LANES:(g + 1) * LANES].T[0:GATE_ROWS]


GATE_ROWS = 16


def _nsa_prep(qn, ks, vs, kw, vw, gates, cos, sin, gq, gk, b, s):
    tm = min(TM_PREP, s)
    nb = s // tm
    row = lambda width: pl.BlockSpec((tm, width), lambda bi, i: (bi * nb + i, 0))
    tab = pl.BlockSpec((tm, LANES), lambda bi, i: (i, 0))
    full = lambda a: pl.BlockSpec(a.shape, lambda bi, i: (0,) * a.ndim)
    g, dh = NSA_KV_GROUPS, NSA_HEAD_DIM
    vt_spec = pl.BlockSpec((1, g, tm // KEY_CHUNK, V_AUG, KEY_CHUNK), lambda bi, i: (bi, 0, i, 0, 0))
    return pl.pallas_call(
        functools.partial(_nsa_prep_kernel, tm=tm),
        out_shape=[jax.ShapeDtypeStruct((b, NSA_HEADS, dh, s), BF16),
                   jax.ShapeDtypeStruct((b, g, s, 2 * dh), BF16),
                   _vt_shape(b, g, s),
                   jax.ShapeDtypeStruct((b, g, s, dh), BF16),
                   _vt_shape(b, g, s),
                   jax.ShapeDtypeStruct((b, g, GATE_ROWS, s), F32)],
        grid=(b, nb),
        in_specs=[row(qn.shape[1]), row(LANES), row(LANES), row(LANES), row(LANES), row(g * LANES),
                  tab, tab, full(gq), full(gk)],
        out_specs=[pl.BlockSpec((1, NSA_HEADS, dh, tm), lambda bi, i: (bi, 0, 0, i)),
                   pl.BlockSpec((1, g, tm, 2 * dh), lambda bi, i: (bi, 0, i, 0)),
                   vt_spec,
                   pl.BlockSpec((1, g, tm, dh), lambda bi, i: (bi, 0, i, 0)),
                   vt_spec,
                   pl.BlockSpec((1, g, GATE_ROWS, tm), lambda bi, i: (bi, 0, 0, i))],
        compiler_params=_params("parallel", "parallel"),
        name="nsa_prep",
    )(qn, ks, vs, kw, vw, gates, cos, sin, gq, gk)


def _compress_kernel(kc_ref, vc_ref, pek_ref, pev_ref, wka_ref, wkb_ref, wk2_ref, wva_ref, wvb_ref, wv2_ref,
                     gk_ref, cos_ref, sin_ref, kcmp_ref, vcmpt_ref):
    def comp(r_ref, pe_ref, wa_ref, wb_ref, w2_ref):
        r = r_ref[0]
        nr = r.shape[0]
        a = _dot((r + pe_ref[0:1, :]).astype(BF16), wa_ref[...])
        bb = _dot((r + pe_ref[1:2, :]).astype(BF16), wb_ref[...])
        hd = a + pltpu.roll(bb, nr - 1, 0)
        hd = hd * jax.nn.sigmoid(hd)
        return _dot(hd.astype(BF16), w2_ref[...])

    k = _pair_norm_rope(comp(kc_ref, pek_ref, wka_ref, wkb_ref, wk2_ref), gk_ref[...], cos_ref[...], sin_ref[...])
    vt = comp(vc_ref, pev_ref, wva_ref, wvb_ref, wv2_ref).T
    dh = NSA_HEAD_DIM
    for g in range(NSA_KV_GROUPS):
        kcmp_ref[0, g] = k[:, g * dh:(g + 1) * dh].astype(kcmp_ref.dtype)
        vcmpt_ref[0, g] = vt[g * dh:(g + 1) * dh].astype(vcmpt_ref.dtype)


def _compress(kc, vc, pek, pev, wka, wkb, wk2, wva, wvb, wv2, gk, cos, sin, b, s):
    nr = s // CMP_STRIDE
    width = CMP_STRIDE * LANES
    kc3 = kc.reshape(b, nr, width)
    vc3 = vc.reshape(b, nr, width)
    full = lambda a: pl.BlockSpec(a.shape, lambda bi: (0,) * a.ndim)
    blk = pl.BlockSpec((1, nr, width), lambda bi: (bi, 0, 0))
    g, dh = NSA_KV_GROUPS, NSA_HEAD_DIM
    return pl.pallas_call(
        _compress_kernel,
        out_shape=[jax.ShapeDtypeStruct((b, g, nr, dh), BF16), jax.ShapeDtypeStruct((b, g, dh, nr), BF16)],
        grid=(b,),
        in_specs=[blk, blk, full(pek), full(pev), full(wka), full(wkb), full(wk2), full(wva), full(wvb), full(wv2),
                  full(gk), full(cos), full(sin)],
        out_specs=[pl.BlockSpec((1, g, nr, dh), lambda bi: (bi, 0, 0, 0)),
                   pl.BlockSpec((1, g, dh, nr), lambda bi: (bi, 0, 0, 0))],
        compiler_params=_params("parallel"),
        name="nsa_compress",
    )(kc3, vc3, pek, pev, wka, wkb, wk2, wva, wvb, wv2, gk, cos, sin)


def _selection_bias(imp, i, tq, n_slc):
    nb = imp.shape[0]
    jrow = lax.broadcasted_iota(jnp.int32, (nb, tq), 0)
    blk_t = (i * tq + lax.broadcasted_iota(jnp.int32, (nb, tq), 1)) // SLC_BLOCK
    forced = (jrow == 0) | (jrow == blk_t) | (jrow == blk_t - 1)
    imp = jnp.where(forced, FORCED_BLOCK_SCORE, imp)
    imp = jnp.where(jrow <= blk_t, imp, -1.0)
    n_slab = -(-n_slc // SUBLANES)
    slabs = [imp[SUBLANES * c:SUBLANES * (c + 1), :] for c in range(n_slab)]
    ranks = [jnp.zeros((SUBLANES, tq), jnp.int32) for _ in range(n_slab)]
    sub = lax.broadcasted_iota(jnp.int32, (SUBLANES, tq), 0)
    for jp in range(n_slc):
        other = imp[jp:jp + 1, :]
        for c in range(n_slab):
            if SUBLANES * c > jp:
                ranks[c] = ranks[c] + jnp.where(other >= slabs[c], 1, 0)
            elif SUBLANES * (c + 1) - 1 <= jp:
                ranks[c] = ranks[c] + jnp.where(other > slabs[c], 1, 0)
            else:
                ge = jnp.where(other >= slabs[c], 1, 0)
                gt = jnp.where(other > slabs[c], 1, 0)
                ranks[c] = ranks[c] + jnp.where(sub > jp - SUBLANES * c, ge, gt)
    if nb > SUBLANES * n_slab:
        ranks.append(jnp.full((nb - SUBLANES * n_slab, tq), SLC_TOP_N, jnp.int32))
    rank = jnp.concatenate(ranks, axis=0)
    return jnp.where(rank < SLC_TOP_N, 0.0, MASK_VALUE)


def _nsa_attn_kernel(qt_ref, kc_ref, vct_ref, ks_ref, vst_ref, kw_ref, vwt_ref, gt_ref, ovl_ref, o_ref,
                     qa_ref, m_ref, acc_ref, *, tq, n_slc):
    i = pl.program_id(1)
    q0 = i * tq
    hpg, dh, ng = NSA_HPG, NSA_HEAD_DIM, NSA_KV_GROUPS
    st = (m_ref, acc_ref)
    heads = [(g, h) for g in range(ng) for h in range(hpg)]

    nc = kc_ref.shape[2]
    crow = lax.broadcasted_iota(jnp.int32, (nc, tq), 0)
    qcol = lax.broadcasted_iota(jnp.int32, (nc, tq), 1)
    cmask = crow * CMP_STRIDE + (CMP_BLOCK - 1) <= q0 + qcol
    scores = [_dot(kc_ref[0, g], qt_ref[0, g * hpg + h]) for g, h in heads]
    probs = []
    for s in scores:
        s = jnp.where(cmask, s, MASK_VALUE)
        p = jnp.where(cmask, jnp.exp2(s - jnp.max(s, axis=0, keepdims=True)), 0.0)
        probs.append(p * (1.0 / jnp.maximum(jnp.sum(p, axis=0, keepdims=True), 1e-30)))
    o_cmp = [_dot(vct_ref[0, g], p.astype(BF16)) for (g, h), p in zip(heads, probs)]

    ovl = ovl_ref[...]
    for g in range(ng):
        ps = probs[g * hpg]
        for h in range(1, hpg):
            ps = ps + probs[g * hpg + h]
        hi = ps.astype(BF16)
        rem = ps - hi.astype(F32)
        mid = rem.astype(BF16)
        low = (rem - mid.astype(F32)).astype(BF16)
        imp = _dot(ovl, hi) + _dot(ovl, mid) + _dot(ovl, low)
        bias = _selection_bias(imp, i, tq, n_slc).astype(BF16)
        for h in range(hpg):
            qa_ref[g * hpg + h, 0:dh, :] = qt_ref[0, g * hpg + h]
            qa_ref[g * hpg + h, dh:2 * dh, :] = bias

    def kchunk(ref, g):
        return lambda c: ref[0, g, pl.ds(pl.multiple_of(c * KEY_CHUNK, KEY_CHUNK), KEY_CHUNK), :]

    _causal_sweep(st, [(lambda n=g * hpg + h: qa_ref[n], kchunk(ks_ref, g), lambda c, g=g: vst_ref[0, g, c])
                       for g, h in heads], i, tq)
    o_slc = [_flash_result(st, n) for n in range(len(heads))]

    _flash_reset(st)
    key = lax.broadcasted_iota(jnp.int32, (KEY_CHUNK, tq), 0)
    qry = lax.broadcasted_iota(jnp.int32, (KEY_CHUNK, tq), 1)
    n_back = WINDOW // KEY_CHUNK

    def win_ops(c, mask):
        return [(g * hpg + h, qt_ref[0, g * hpg + h], kchunk(kw_ref, g)(c), vwt_ref[0, g, c], mask) for g, h in heads]

    for d in range(n_back, 0, -1):
        @pl.when(i >= d)
        def _(d=d):
            _flash_steps(st, win_ops(i - d, (key > qry) if d == n_back else None))
    _flash_steps(st, win_ops(i, key <= qry))
    o_win = [_flash_result(st, n) for n in range(len(heads))]

    outs = []
    for g in range(ng):
        sg = jax.nn.sigmoid(gt_ref[0, g])
        for h in range(hpg):
            n = g * hpg + h
            outs.append(sg[3 * h:3 * h + 1] * o_cmp[n] + sg[3 * h + 1:3 * h + 2] * o_slc[n]
                        + sg[3 * h + 2:3 * h + 3] * o_win[n])
    o_ref[0] = jnp.concatenate(outs, axis=0).T.astype(o_ref.dtype)


def _nsa_attn(qt, kcmp, vcmpt, kslc, vslct, kwin, vwint, gatet, ovl, b, s):
    tq = TQ_ATTN
    assert tq == KEY_CHUNK and s % tq == 0 and WINDOW % KEY_CHUNK == 0
    n_slc = s // SLC_BLOCK
    assert n_slc <= NSA_HEAD_DIM
    g, dh = NSA_KV_GROUPS, NSA_HEAD_DIM
    nc = kcmp.shape[2]
    whole = lambda a: pl.BlockSpec((1,) + a.shape[1:], lambda bi, i: (bi,) + (0,) * (a.ndim - 1))
    return pl.pallas_call(
        functools.partial(_nsa_attn_kernel, tq=tq, n_slc=n_slc),
        out_shape=jax.ShapeDtypeStruct((b, s, NSA_HEADS * dh), BF16),
        grid=(b, s // tq),
        in_specs=[pl.BlockSpec((1, NSA_HEADS, dh, tq), lambda bi, i: (bi, 0, 0, i)),
                  whole(kcmp), whole(vcmpt), whole(kslc), whole(vslct), whole(kwin), whole(vwint),
                  pl.BlockSpec((1, g, GATE_ROWS, tq), lambda bi, i: (bi, 0, 0, i)),
                  pl.BlockSpec(ovl.shape, lambda bi, i: (0, 0))],
        out_specs=pl.BlockSpec((1, tq, NSA_HEADS * dh), lambda bi, i: (bi, i, 0)),
        scratch_shapes=[pltpu.VMEM((NSA_HEADS, 2 * dh, tq), BF16)] + _flash_scratch(NSA_HEADS, tq),
        compiler_params=_params("parallel", "arbitrary"),
        name="nsa_attn",
    )(qt, kcmp, vcmpt, kslc, vslct, kwin, vwint, gatet, ovl)


def _merge_kernel(x_ref, oa_ref, cv_ref, on_ref, gm_ref, woa_ref, wcv_ref, bcv_ref, won_ref, wout_ref, o_ref):
    d = x_ref.shape[1]
    o_a = _dot(oa_ref[...], woa_ref[...])
    o_b = _dot(cv_ref[...], wcv_ref[...]) + bcv_ref[...]
    o_c = _dot(on_ref[...], won_ref[...])
    mix = (jax.nn.sigmoid(gm_ref[:, 0:d]) * o_a + jax.nn.sigmoid(gm_ref[:, d:2 * d]) * o_b
           + jax.nn.sigmoid(gm_ref[:, 2 * d:3 * d]) * o_c)
    o_ref[...] = x_ref[...] + _dot(mix.astype(BF16), wout_ref[...])


def _merge(x2d, oa, cv, on, gm, woa, wcv, bcv, won, wout):
    t, d = x2d.shape
    tm = min(TM_MERGE, t)
    row = lambda a: pl.BlockSpec((tm, a.shape[1]), lambda i: (i, 0))
    full = lambda a: pl.BlockSpec(a.shape, lambda i: (0,) * a.ndim)
    return pl.pallas_call(
        _merge_kernel,
        out_shape=jax.ShapeDtypeStruct((t, d), F32),
        grid=(t // tm,),
        in_specs=[row(x2d), row(oa), row(cv), row(on), row(gm), full(woa), full(wcv), full(bcv), full(won), full(wout)],
        out_specs=pl.BlockSpec((tm, d), lambda i: (i, 0)),
        compiler_params=_params("parallel"),
        name="merge",
    )(x2d, oa, cv, on, gm, woa, wcv, bcv, won, wout)


def _ffn_kernel(x_ref, g_ref, w1_ref, w2_ref, o_ref, h_ref, acc_ref):
    j = pl.program_id(1)

    @pl.when(j == 0)
    def _():
        x = x_ref[...]
        h_ref[...] = _rms(x, g_ref[...]).astype(h_ref.dtype)
        acc_ref[...] = x

    a = jnp.maximum(_dot(h_ref[...], w1_ref[...]), 0.0)
    acc_ref[...] += _dot((a * a).astype(BF16), w2_ref[...])

    @pl.when(j == pl.num_programs(1) - 1)
    def _():
        o_ref[...] = acc_ref[...]


def _ffn(x2d, g, w1, w2):
    t, d = x2d.shape
    f = w1.shape[1]
    tm = min(TM_FFN, t)
    tf = min(TF_FFN, f)
    return pl.pallas_call(
        _ffn_kernel,
        out_shape=jax.ShapeDtypeStruct((t, d), F32),
        grid=(t // tm, f // tf),
        in_specs=[pl.BlockSpec((tm, d), lambda i, j: (i, 0)),
                  pl.BlockSpec((1, d), lambda i, j: (0, 0)),
                  pl.BlockSpec((d, tf), lambda i, j: (0, j)),
                  pl.BlockSpec((tf, d), lambda i, j: (j, 0))],
        out_specs=pl.BlockSpec((tm, d), lambda i, j: (i, 0)),
        scratch_shapes=[pltpu.VMEM((tm, d), BF16), pltpu.VMEM((tm, d), F32)],
        compiler_params=_params("parallel", "arbitrary"),
        name="ffn",
    )(x2d, g, w1, w2)


def _rope_tables(pos, rot_dim, group, width=LANES):
    half = rot_dim // 2
    inv = jnp.power(jnp.float32(ROPE_THETA), -jnp.arange(half, dtype=jnp.float32) * (2.0 / rot_dim))
    ang = pos.astype(jnp.float32)[:, None] * inv
    n = ang.shape[0]
    cos = jnp.concatenate([jnp.cos(ang), jnp.cos(ang), jnp.ones((n, group - rot_dim), F32)], axis=1)
    sin = jnp.concatenate([-jnp.sin(ang), jnp.sin(ang), jnp.zeros((n, group - rot_dim), F32)], axis=1)
    reps = width // group
    return jnp.tile(cos, (1, reps)), jnp.tile(sin, (1, reps))


def _pad_cols(w, width):
    return jnp.pad(w, ((0, 0), (0, width - w.shape[1])))


def _overlap_t(s):
    n_cmp_pad = s // CMP_STRIDE
    n_slc = s // SLC_BLOCK
    c_start = np.arange(n_cmp_pad) * CMP_STRIDE
    s_start = np.arange(n_slc) * SLC_BLOCK
    ov = ((c_start[None, :] < s_start[:, None] + SLC_BLOCK) & (c_start[None, :] + CMP_BLOCK > s_start[:, None]))
    out = np.zeros((NSA_HEAD_DIM, n_cmp_pad), np.float32)
    out[:n_slc] = ov
    return jnp.asarray(out, BF16)


def _compress_weights(w1, w2):
    g, dh, hid = NSA_KV_GROUPS, NSA_HEAD_DIM, CMP_HIDDEN
    w1r = w1.reshape(2, CMP_STRIDE, dh, hid)
    wide = jnp.zeros((2, CMP_STRIDE, g, dh, g, hid), w1.dtype)
    for gi in range(g):
        wide = wide.at[:, :, gi, :, gi, :].set(w1r)
    wide = wide.reshape(2, CMP_STRIDE * g * dh, g * hid).astype(BF16)
    w2w = jnp.zeros((g, hid, g, dh), w2.dtype)
    for gi in range(g):
        w2w = w2w.at[gi, :, gi, :].set(w2)
    return wide[0], wide[1], w2w.reshape(g * hid, g * dh).astype(BF16)


def _compress_pe(pe):
    pe2 = pe.reshape(2, CMP_STRIDE, 1, NSA_HEAD_DIM)
    return jnp.broadcast_to(pe2, (2, CMP_STRIDE, NSA_KV_GROUPS, NSA_HEAD_DIM)).reshape(2, -1).astype(F32)


def kernel(x, g_mix, w_in, g_cq, g_ckv, w_uq, w_ukv, g_q_mla, g_k_mla, w_o_mla, b_glu, w_dw, b_dw, g_conv_ln, b_conv_ln, w_conv_out, b_conv_out, pe_cmp_k, pe_cmp_v, w_cmp_k1, w_cmp_k2, w_cmp_v1, w_cmp_v2, g_q_nsa, g_k_nsa, w_o_nsa, w_out, g_ffn, w_ff1, w_ff2):
    b, s, d = x.shape
    depth = w_in.shape[0]
    t = b * s
    q_rank, kv_rank = g_cq.shape[1], g_ckv.shape[1]
    nh, hd = NSA_HEADS, NSA_HEAD_DIM
    ngd = NSA_KV_GROUPS * hd
    assert ngd == LANES and s % CMP_STRIDE == 0

    pos = jnp.arange(s)
    cos_mla, sin_mla = _rope_tables(pos, MLA_ROPE_DIM, LANES)
    cos_nsa, sin_nsa = _rope_tables(pos, NSA_ROPE_DIM, hd)
    cmp_pos = jnp.arange(s // CMP_STRIDE) * CMP_STRIDE + CMP_BLOCK - 1
    cos_cmp, sin_cmp = _rope_tables(cmp_pos, NSA_ROPE_DIM, hd)
    ovl = _overlap_t(s)

    splits = (q_rank, kv_rank, MLA_ROPE_DIM, 2 * CONV_CH, nh * hd, 6 * ngd, 3 * nh, 3 * d)
    offs = np.concatenate([[0], np.cumsum(splits)])
    widths = (q_rank, kv_rank, LANES, 2 * CONV_CH, nh * hd) + (LANES,) * 6 + (NSA_KV_GROUPS * LANES, 3 * d)

    x2d = x.reshape(t, d)
    for l in range(depth):
        wl = w_in[l]
        seg = [wl[:, offs[k]:offs[k + 1]] for k in range(len(splits))]
        gate_cols = [_pad_cols(seg[6][:, gi * NSA_HPG * 3:(gi + 1) * NSA_HPG * 3], LANES) for gi in range(NSA_KV_GROUPS)]
        w_in_p = jnp.concatenate(
            [seg[0], seg[1], _pad_cols(seg[2], LANES), seg[3], seg[4], seg[5]] + gate_cols + [seg[7]], axis=1).astype(BF16)
        (cq, ckv, kr, u2, qn, kc, vc, ks, vs, kw, vw, gn, gm) = _inproj(x2d, g_mix[l][None, :], w_in_p, widths)

        wuq = w_uq[l].reshape(q_rank, MLA_HEADS, MLA_QK_DIM)
        wuq = jnp.pad(wuq, ((0, 0), (0, 0), (0, LANES - MLA_QK_DIM))).reshape(q_rank, MLA_HEADS * LANES).astype(BF16)
        wukv = w_ukv[l].reshape(kv_rank, MLA_HEADS, MLA_NOPE_DIM + V_DIM)
        wuk = jnp.pad(wukv[:, :, :MLA_NOPE_DIM], ((0, 0), (0, 0), (MLA_ROPE_DIM, LANES - MLA_QK_DIM)))
        wuk = wuk.reshape(kv_rank, MLA_HEADS * LANES).astype(BF16)
        wuv = wukv[:, :, MLA_NOPE_DIM:].reshape(kv_rank, MLA_HEADS * V_DIM).astype(BF16)
        gq = _pad_cols((g_q_mla[l] * (MLA_QK_DIM ** -0.5 * LOG2_E))[None, :], LANES)
        gk = _pad_cols(g_k_mla[l][None, :], LANES)
        qt_m, k_m, vt_m = _mla_prep(cq, ckv, kr, cos_mla, sin_mla, g_cq[l][None, :], g_ckv[l][None, :],
                                    wuq, wuk, wuv, gq, gk, b, s)
        o_mla = _mla_attn(qt_m, k_m, vt_m).reshape(t, MLA_HEADS * V_DIM)

        wdw = jnp.pad(w_dw[l], ((0, CONV_HALO - CONV_WIDTH), (0, 0)))
        o_cv = _conv(u2, b_glu[l][None, :], wdw, b_dw[l][None, :], g_conv_ln[l][None, :], b_conv_ln[l][None, :],
                     b, s).reshape(t, CONV_CH)

        gqn = jnp.tile(g_q_nsa[l] * (hd ** -0.5 * LOG2_E), 2)[None, :]
        gkn = jnp.tile(g_k_nsa[l], 2)[None, :]
        qt_n, k_slc, vt_slc, k_win, vt_win, gate_t = _nsa_prep(qn, ks, vs, kw, vw, gn, cos_nsa, sin_nsa, gqn, gkn, b, s)
        wka, wkb, wk2 = _compress_weights(w_cmp_k1[l], w_cmp_k2[l])
        wva, wvb, wv2 = _compress_weights(w_cmp_v1[l], w_cmp_v2[l])
        k_cmp, vt_cmp = _compress(kc, vc, _compress_pe(pe_cmp_k[l]), _compress_pe(pe_cmp_v[l]),
                                  wka, wkb, wk2, wva, wvb, wv2, gkn, cos_cmp, sin_cmp, b, s)
        o_nsa = _nsa_attn(qt_n, k_cmp, vt_cmp, k_slc, vt_slc, k_win, vt_win, gate_t, ovl, b, s).reshape(t, nh * hd)

        x2d = _merge(x2d, o_mla, o_cv, o_nsa, gm, w_o_mla[l].astype(BF16), w_conv_out[l].astype(BF16),
                     b_conv_out[l][None, :], w_o_nsa[l].astype(BF16), w_out[l].astype(BF16))
        x2d = _ffn(x2d, g_ffn[l][None, :], w1=w_ff1[l].astype(BF16), w2=w_ff2[l].astype(BF16))
    return x2d.reshape(b, s, d)
```

```python
import functools
import math

import numpy as np
import jax
import jax.numpy as jnp
from jax import lax
from jax.experimental import pallas as pl
from jax.experimental.pallas import tpu as pltpu

F32 = jnp.float32
BF16 = jnp.bfloat16

ROPE_THETA = 500000.0
NORM_EPS = 1e-6
MLA_HEADS = 8
MLA_NOPE_DIM = 64
MLA_ROPE_DIM = 32
MLA_QK_DIM = MLA_NOPE_DIM + MLA_ROPE_DIM
CONV_CH = 512
CONV_WIDTH = 31
NSA_HEADS = 8
NSA_KV_GROUPS = 2
NSA_HEAD_DIM = 64
NSA_ROPE_DIM = NSA_HEAD_DIM // 4
NSA_HPG = NSA_HEADS // NSA_KV_GROUPS
CMP_BLOCK = 32
CMP_STRIDE = 16
CMP_HIDDEN = 128
SLC_BLOCK = 64
SLC_TOP_N = 16
WINDOW = 512
FORCED_BLOCK_SCORE = 1e6
V_DIM = 64

LANES = 128
SUBLANES = 8
BF16_SUBLANES = 16
VMEM_LIMIT_BYTES = 56 * 1024 * 1024

MASK_VALUE = -1e30
LOG2_E = math.log2(math.e)
V_AUG = V_DIM + BF16_SUBLANES

TM_PROJ = 256
TM_PREP = 512
TQ_ATTN = 256
KEY_CHUNK = 256
TS_CONV = 256
CONV_HALO = 32
CONV_CHUNK = 64
TM_MERGE = 512
TM_FFN = 1024
TF_FFN = 512


def _dot(a, b):
    return jnp.dot(a, b, preferred_element_type=F32)


def _rms(x, g):
    return x * lax.rsqrt(jnp.mean(x * x, axis=-1, keepdims=True) + NORM_EPS) * g


def _params(*sem):
    return pltpu.CompilerParams(dimension_semantics=sem, vmem_limit_bytes=VMEM_LIMIT_BYTES)


def _flash_scratch(n_chains, n):
    return [pltpu.VMEM((n_chains, 1, n), F32), pltpu.VMEM((n_chains, V_AUG, n), F32)]


def _flash_reset(st):
    m_ref, acc_ref = st
    m_ref[...] = jnp.full(m_ref.shape, MASK_VALUE, F32)
    acc_ref[...] = jnp.zeros(acc_ref.shape, F32)


def _flash_steps(st, ops):
    m_ref, acc_ref = st
    scores = [_dot(op[2], op[1]) for op in ops]
    probs = []
    for (n, _, _, _, mask, bias), s in zip(ops, scores):
        if mask is not None:
            s = jnp.where(mask, s, MASK_VALUE)
        if bias is not None:
            s = s + bias
        m = m_ref[n]
        m_new = jnp.maximum(m, jnp.max(s, axis=0, keepdims=True))
        m_ref[n] = m_new
        probs.append((jnp.exp2(m - m_new), jnp.exp2(s - m_new).astype(BF16)))
    pvs = [_dot(op[3], p) for op, (_, p) in zip(ops, probs)]
    for op, (alpha, _), pv in zip(ops, probs, pvs):
        acc_ref[op[0]] = alpha * acc_ref[op[0]] + pv


def _flash_result(st, n):
    _, acc_ref = st
    return acc_ref[n, 0:V_DIM, :] / acc_ref[n, V_DIM:V_DIM + 1, :]


def _causal_sweep(st, chains, i, tq, extra_ops_fn=None):
    _flash_reset(st)

    def ops_for(c, mask):
        return [(n, qt_fn(), k_fn(c), vt_fn(c), mask, None) for n, (qt_fn, k_fn, vt_fn) in enumerate(chains)]

    def pair(j, carry):
        _flash_steps(st, ops_for(2 * j, None) + ops_for(2 * j + 1, None))
        return carry

    lax.fori_loop(0, i // 2, pair, 0)
    key = lax.broadcasted_iota(jnp.int32, (KEY_CHUNK, tq), 0)
    qry = lax.broadcasted_iota(jnp.int32, (KEY_CHUNK, tq), 1)
    extra = extra_ops_fn if extra_ops_fn is not None else (lambda: [])

    @pl.when(i % 2 == 1)
    def _():
        _flash_steps(st, ops_for(i - 1, None) + ops_for(i, key <= qry) + extra())

    @pl.when(i % 2 == 0)
    def _():
        _flash_steps(st, ops_for(i, key <= qry) + extra())


def _ones_rows(n):
    return jnp.where(lax.broadcasted_iota(jnp.int32, (V_AUG - V_DIM, n), 0) == 0, 1.0, 0.0).astype(BF16)


def _store_vt(v, vt_ref, n_heads):
    ones = _ones_rows(KEY_CHUNK)
    for p in range(n_heads // 2):
        t = v[:, p * LANES:(p + 1) * LANES].T.astype(vt_ref.dtype)
        for c in range(vt_ref.shape[2]):
            for e in range(2):
                vt_ref[0, 2 * p + e, c, 0:V_DIM, :] = t[e * V_DIM:(e + 1) * V_DIM, c * KEY_CHUNK:(c + 1) * KEY_CHUNK]
                vt_ref[0, 2 * p + e, c, V_DIM:V_AUG, :] = ones


def _vt_shape(b, n, s):
    return jax.ShapeDtypeStruct((b, n, s // KEY_CHUNK, V_AUG, KEY_CHUNK), BF16)


def _inproj_kernel(x_ref, g_ref, w_ref, *out_refs, widths):
    h = _rms(x_ref[...], g_ref[...]).astype(BF16)
    off = 0
    for o_ref, wd in zip(out_refs, widths):
        o_ref[...] = _dot(h, w_ref[:, off:off + wd]).astype(o_ref.dtype)
        off += wd


def _inproj(x2d, g, w, widths):
    t, d = x2d.shape
    n = w.shape[1]
    return pl.pallas_call(
        functools.partial(_inproj_kernel, widths=widths),
        out_shape=[jax.ShapeDtypeStruct((t, wd), F32) for wd in widths],
        grid=(t // TM_PROJ,),
        in_specs=[pl.BlockSpec((TM_PROJ, d), lambda i: (i, 0)),
                  pl.BlockSpec((1, d), lambda i: (0, 0)),
                  pl.BlockSpec((d, n), lambda i: (0, 0), pipeline_mode=pl.Buffered(1))],
        out_specs=[pl.BlockSpec((TM_PROJ, wd), lambda i: (i, 0)) for wd in widths],
        compiler_params=_params("parallel"),
        name="inproj",
    )(x2d, g, w)


def _mla_prep_kernel(cq_ref, ckv_ref, kr_ref, krot_ref, cos_ref, sin_ref, gcq_ref, gckv_ref,
                     wuq_ref, wuk_ref, wuv_ref, gq_ref, gk_ref, qt_ref, k_ref, vt_ref):
    cos = cos_ref[...]
    sin = sin_ref[...]
    q_cos, q_sin = gq_ref[0:1, :] * cos, gq_ref[1:2, :] * sin
    k_cos, k_sin = gk_ref[0:1, :] * cos, gk_ref[1:2, :] * sin
    inv_dim = 1.0 / MLA_QK_DIM
    hw = MLA_HEADS * LANES
    qq = _dot(_rms(cq_ref[...], gcq_ref[...]).astype(BF16), wuq_ref[...])
    for h in range(MLA_HEADS):
        qh = qq[:, h * LANES:(h + 1) * LANES]
        qp = qq[:, hw + h * LANES:hw + (h + 1) * LANES]
        r = lax.rsqrt(jnp.sum(qh * qh, axis=-1, keepdims=True) * inv_dim + NORM_EPS)
        qt_ref[0, h] = ((qh * q_cos + qp * q_sin) * r).T.astype(qt_ref.dtype)

    ckvn = _rms(ckv_ref[...], gckv_ref[...]).astype(BF16)
    kn = _dot(ckvn, wuk_ref[...])
    v = _dot(ckvn, wuv_ref[...])
    kr = kr_ref[...]
    kr_rot = krot_ref[...] * k_sin
    for h in range(MLA_HEADS):
        kh = kn[:, h * LANES:(h + 1) * LANES] + kr
        r = lax.rsqrt(jnp.sum(kh * kh, axis=-1, keepdims=True) * inv_dim + NORM_EPS)
        k_ref[0, h] = ((kh * k_cos + kr_rot) * r).astype(k_ref.dtype)
    _store_vt(v, vt_ref, MLA_HEADS)


def _mla_prep(cq, ckv, kr, krot, cos, sin, gcq, gckv, wuq, wuk, wuv, gq, gk, b, s):
    tm = min(TM_PREP, s)
    nb = s // tm
    row = lambda width: pl.BlockSpec((tm, width), lambda bi, i: (bi * nb + i, 0))
    tab = pl.BlockSpec((tm, LANES), lambda bi, i: (i, 0))
    full = lambda a: pl.BlockSpec(a.shape, lambda bi, i: (0,) * a.ndim)
    return pl.pallas_call(
        _mla_prep_kernel,
        out_shape=[jax.ShapeDtypeStruct((b, MLA_HEADS, LANES, s), BF16),
                   jax.ShapeDtypeStruct((b, MLA_HEADS, s, LANES), BF16),
                   _vt_shape(b, MLA_HEADS, s)],
        grid=(b, nb),
        in_specs=[row(cq.shape[1]), row(ckv.shape[1]), row(LANES), row(LANES), tab, tab,
                  full(gcq), full(gckv), full(wuq), full(wuk), full(wuv), full(gq), full(gk)],
        out_specs=[pl.BlockSpec((1, MLA_HEADS, LANES, tm), lambda bi, i: (bi, 0, 0, i)),
                   pl.BlockSpec((1, MLA_HEADS, tm, LANES), lambda bi, i: (bi, 0, i, 0)),
                   pl.BlockSpec((1, MLA_HEADS, tm // KEY_CHUNK, V_AUG, KEY_CHUNK), lambda bi, i: (bi, 0, i, 0, 0))],
        compiler_params=_params("parallel", "parallel"),
        name="mla_prep",
    )(cq, ckv, kr, krot, cos, sin, gcq, gckv, wuq, wuk, wuv, gq, gk)


def _mla_attn_kernel(qt_ref, k_ref, vt_ref, o_ref, m_ref, acc_ref, *, tq):
    i = pl.program_id(1)
    st = (m_ref, acc_ref)
    chains = [(lambda h=h: qt_ref[0, h],
               lambda c, h=h: k_ref[0, h, pl.ds(pl.multiple_of(c * KEY_CHUNK, KEY_CHUNK), KEY_CHUNK), :],
               lambda c, h=h: vt_ref[0, h, c]) for h in range(MLA_HEADS)]
    _causal_sweep(st, chains, i, tq)
    outs = [_flash_result(st, h) for h in range(MLA_HEADS)]
    o_ref[0] = jnp.concatenate(outs, axis=0).T.astype(o_ref.dtype)


def _mla_attn(qt, k, vt):
    b, nh, s, _ = k.shape
    tq = TQ_ATTN
    assert tq == KEY_CHUNK and s % tq == 0
    return pl.pallas_call(
        functools.partial(_mla_attn_kernel, tq=tq),
        out_shape=jax.ShapeDtypeStruct((b, s, nh * V_DIM), BF16),
        grid=(b, s // tq),
        in_specs=[pl.BlockSpec((1, nh, LANES, tq), lambda bi, i: (bi, 0, 0, i)),
                  pl.BlockSpec((1, nh, s, LANES), lambda bi, i: (bi, 0, 0, 0)),
                  pl.BlockSpec((1, nh, s // KEY_CHUNK, V_AUG, KEY_CHUNK), lambda bi, i: (bi, 0, 0, 0, 0))],
        out_specs=pl.BlockSpec((1, tq, nh * V_DIM), lambda bi, i: (bi, i, 0)),
        scratch_shapes=_flash_scratch(nh, tq),
        compiler_params=_params("parallel", "arbitrary"),
        name="mla_attn",
    )(qt, k, vt)


def _conv_kernel(u_ref, halo_ref, bglu_ref, wdw_ref, bdw_ref, gln_ref, bln_ref, o_ref, ext_ref, *, ts):
    i = pl.program_id(1)
    bglu = bglu_ref[...]

    def glu(u2):
        u2 = u2 + bglu
        return u2[:, :CONV_CH] * jax.nn.sigmoid(u2[:, CONV_CH:])

    ext_ref[0, 0:CONV_HALO, :] = jnp.where(i > 0, glu(halo_ref[0]), 0.0)
    ext_ref[0, CONV_HALO:, :] = glu(u_ref[0])
    rows = ts + CONV_HALO - SUBLANES
    for b in range(1, SUBLANES):
        ext_ref[b, 0:rows, :] = ext_ref[0, b:b + rows, :]
    wdw = wdw_ref[...]
    shift = CONV_HALO - (CONV_WIDTH - 1)
    for r0 in range(0, ts, CONV_CHUNK):
        acc = jnp.zeros((CONV_CHUNK, CONV_CH), F32)
        for k in range(CONV_WIDTH):
            a, b = divmod(shift + k, SUBLANES)
            acc = acc + wdw[k:k + 1, :] * ext_ref[b, r0 + SUBLANES * a:r0 + SUBLANES * a + CONV_CHUNK, :]
        acc = acc + bdw_ref[...]
        mu = jnp.mean(acc, axis=-1, keepdims=True)
        d = acc - mu
        var = jnp.mean(d * d, axis=-1, keepdims=True)
        y = d * lax.rsqrt(var + NORM_EPS) * gln_ref[...] + bln_ref[...]
        o_ref[0, r0:r0 + CONV_CHUNK, :] = (y * jax.nn.sigmoid(y)).astype(o_ref.dtype)


def _conv(u2, bglu, wdw, bdw, gln, bln, b, s):
    ts = min(TS_CONV, s)
    u3 = u2.reshape(b, s, 2 * CONV_CH)
    hb = ts // CONV_HALO
    full = lambda a: pl.BlockSpec(a.shape, lambda bi, i: (0,) * a.ndim)
    return pl.pallas_call(
        functools.partial(_conv_kernel, ts=ts),
        out_shape=jax.ShapeDtypeStruct((b, s, CONV_CH), BF16),
        grid=(b, s // ts),
        in_specs=[pl.BlockSpec((1, ts, 2 * CONV_CH), lambda bi, i: (bi, i, 0)),
                  pl.BlockSpec((1, CONV_HALO, 2 * CONV_CH), lambda bi, i: (bi, jnp.maximum(i * hb - 1, 0), 0)),
                  full(bglu), full(wdw), full(bdw), full(gln), full(bln)],
        out_specs=pl.BlockSpec((1, ts, CONV_CH), lambda bi, i: (bi, i, 0)),
        scratch_shapes=[pltpu.VMEM((SUBLANES, ts + CONV_HALO, CONV_CH), F32)],
        compiler_params=_params("parallel", "parallel"),
        name="conv_mixer",
    )(u3, u3, bglu, wdw, bdw, gln, bln)


def _pair_norm_rope(x, g, cos, sin):
    lane = lax.broadcasted_iota(jnp.int32, x.shape, 1)
    lo = lane < NSA_HEAD_DIM
    x2 = x * x
    s_lo = jnp.sum(jnp.where(lo, x2, 0.0), axis=-1, keepdims=True)
    s_hi = jnp.sum(jnp.where(lo, 0.0, x2), axis=-1, keepdims=True)
    ss = jnp.where(lo, s_lo, s_hi) * (1.0 / NSA_HEAD_DIM)
    y = x * lax.rsqrt(ss + NORM_EPS) * g
    half = NSA_ROPE_DIM // 2
    first = (lane & (NSA_HEAD_DIM - 1)) < half
    rot = jnp.where(first, pltpu.roll(y, LANES - half, 1), pltpu.roll(y, half, 1))
    return y * cos + rot * sin


def _nsa_prep_kernel(qn_ref, ks_ref, vs_ref, kw_ref, vw_ref, gate_ref, cos_ref, sin_ref, gq_ref, gk_ref,
                     qt_ref, kslc_ref, vslct_ref, kwin_ref, vwint_ref, gatet_ref, *, tm):
    i = pl.program_id(1)
    cos = cos_ref[...]
    sin = sin_ref[...]
    dh = NSA_HEAD_DIM
    for p in range(NSA_HEADS // 2):
        yt = _pair_norm_rope(qn_ref[:, p * LANES:(p + 1) * LANES], gq_ref[...], cos, sin).T.astype(qt_ref.dtype)
        qt_ref[0, 2 * p] = yt[0:dh]
        qt_ref[0, 2 * p + 1] = yt[dh:2 * dh]
    ks = _pair_norm_rope(ks_ref[...], gk_ref[...], cos, sin).astype(kslc_ref.dtype)
    kw = _pair_norm_rope(kw_ref[...], gk_ref[...], cos, sin).astype(kwin_ref.dtype)
    blk = (i * tm + lax.broadcasted_iota(jnp.int32, (tm, dh), 0)) // SLC_BLOCK
    onehot = jnp.where(lax.broadcasted_iota(jnp.int32, (tm, dh), 1) == blk, 1.0, 0.0).astype(kslc_ref.dtype)
    for g in range(NSA_KV_GROUPS):
        kslc_ref[0, g, :, 0:dh] = ks[:, g * dh:(g + 1) * dh]
        kslc_ref[0, g, :, dh:2 * dh] = onehot
        kwin_ref[0, g] = kw[:, g * dh:(g + 1) * dh]
    _store_vt(vs_ref[...], vslct_ref, NSA_KV_GROUPS)
    _store_vt(vw_ref[...], vwint_ref, NSA_KV_GROUPS)
    for g in range(NSA_KV_GROUPS):
        gatet_ref[0, g] = gate_ref[:, g * LANES:(g + 1) * LANES].T[0:GATE_ROWS]


GATE_ROWS = 16


def _nsa_prep(qn, ks, vs, kw, vw, gates, cos, sin, gq, gk, b, s):
    tm = min(TM_PREP, s)
    nb = s // tm
    row = lambda width: pl.BlockSpec((tm, width), lambda bi, i: (bi * nb + i, 0))
    tab = pl.BlockSpec((tm, LANES), lambda bi, i: (i, 0))
    full = lambda a: pl.BlockSpec(a.shape, lambda bi, i: (0,) * a.ndim)
    g, dh = NSA_KV_GROUPS, NSA_HEAD_DIM
    vt_spec = pl.BlockSpec((1, g, tm // KEY_CHUNK, V_AUG, KEY_CHUNK), lambda bi, i: (bi, 0, i, 0, 0))
    return pl.pallas_call(
        functools.partial(_nsa_prep_kernel, tm=tm),
        out_shape=[jax.ShapeDtypeStruct((b, NSA_HEADS, dh, s), BF16),
                   jax.ShapeDtypeStruct((b, g, s, 2 * dh), BF16),
                   _vt_shape(b, g, s),
                   jax.ShapeDtypeStruct((b, g, s, dh), BF16),
                   _vt_shape(b, g, s),
                   jax.ShapeDtypeStruct((b, g, GATE_ROWS, s), F32)],
        grid=(b, nb),
        in_specs=[row(qn.shape[1]), row(LANES), row(LANES), row(LANES), row(LANES), row(g * LANES),
                  tab, tab, full(gq), full(gk)],
        out_specs=[pl.BlockSpec((1, NSA_HEADS, dh, tm), lambda bi, i: (bi, 0, 0, i)),
                   pl.BlockSpec((1, g, tm, 2 * dh), lambda bi, i: (bi, 0, i, 0)),
                   vt_spec,
                   pl.BlockSpec((1, g, tm, dh), lambda bi, i: (bi, 0, i, 0)),
                   vt_spec,
                   pl.BlockSpec((1, g, GATE_ROWS, tm), lambda bi, i: (bi, 0, 0, i))],
        compiler_params=_params("parallel", "parallel"),
        name="nsa_prep",
    )(qn, ks, vs, kw, vw, gates, cos, sin, gq, gk)


def _compress_kernel(kc_ref, vc_ref, pek_ref, pev_ref, wka_ref, wkb_ref, wk2_ref, wva_ref, wvb_ref, wv2_ref,
                     gk_ref, cos_ref, sin_ref, kcmp_ref, vcmpt_ref):
    def comp(r_ref, pe_ref, wa_ref, wb_ref, w2_ref):
        r = r_ref[0]
        nr = r.shape[0]
        a = _dot((r + pe_ref[0:1, :]).astype(BF16), wa_ref[...])
        bb = _dot((r + pe_ref[1:2, :]).astype(BF16), wb_ref[...])
        hd = a + pltpu.roll(bb, nr - 1, 0)
        hd = hd * jax.nn.sigmoid(hd)
        return _dot(hd.astype(BF16), w2_ref[...])

    k = _pair_norm_rope(comp(kc_ref, pek_ref, wka_ref, wkb_ref, wk2_ref), gk_ref[...], cos_ref[...], sin_ref[...])
    vt = comp(vc_ref, pev_ref, wva_ref, wvb_ref, wv2_ref).T
    dh = NSA_HEAD_DIM
    for g in range(NSA_KV_GROUPS):
        kcmp_ref[0, g] = k[:, g * dh:(g + 1) * dh].astype(kcmp_ref.dtype)
        vcmpt_ref[0, g] = vt[g * dh:(g + 1) * dh].astype(vcmpt_ref.dtype)


def _compress(kc, vc, pek, pev, wka, wkb, wk2, wva, wvb, wv2, gk, cos, sin, b, s):
    nr = s // CMP_STRIDE
    width = CMP_STRIDE * LANES
    kc3 = kc.reshape(b, nr, width)
    vc3 = vc.reshape(b, nr, width)
    full = lambda a: pl.BlockSpec(a.shape, lambda bi: (0,) * a.ndim)
    blk = pl.BlockSpec((1, nr, width), lambda bi: (bi, 0, 0))
    g, dh = NSA_KV_GROUPS, NSA_HEAD_DIM
    return pl.pallas_call(
        _compress_kernel,
        out_shape=[jax.ShapeDtypeStruct((b, g, nr, dh), BF16), jax.ShapeDtypeStruct((b, g, dh, nr), BF16)],
        grid=(b,),
        in_specs=[blk, blk, full(pek), full(pev), full(wka), full(wkb), full(wk2), full(wva), full(wvb), full(wv2),
                  full(gk), full(cos), full(sin)],
        out_specs=[pl.BlockSpec((1, g, nr, dh), lambda bi: (bi, 0, 0, 0)),
                   pl.BlockSpec((1, g, dh, nr), lambda bi: (bi, 0, 0, 0))],
        compiler_params=_params("parallel"),
        name="nsa_compress",
    )(kc3, vc3, pek, pev, wka, wkb, wk2, wva, wvb, wv2, gk, cos, sin)


def _selection_bias(imp, i, tq, n_slc):
    nb = imp.shape[0]
    jrow = lax.broadcasted_iota(jnp.int32, (nb, tq), 0)
    blk_t = (i * tq + lax.broadcasted_iota(jnp.int32, (nb, tq), 1)) // SLC_BLOCK
    forced = (jrow == 0) | (jrow == blk_t) | (jrow == blk_t - 1)
    imp = jnp.where(forced, FORCED_BLOCK_SCORE, imp)
    imp = jnp.where(jrow <= blk_t, imp, -1.0)
    n_slab = -(-n_slc // SUBLANES)
    slabs = [imp[SUBLANES * c:SUBLANES * (c + 1), :] for c in range(n_slab)]
    ranks = [jnp.zeros((SUBLANES, tq), jnp.int32) for _ in range(n_slab)]
    sub = lax.broadcasted_iota(jnp.int32, (SUBLANES, tq), 0)
    for jp in range(n_slc):
        other = imp[jp:jp + 1, :]
        for c in range(n_slab):
            if SUBLANES * c > jp:
                ranks[c] = ranks[c] + jnp.where(other >= slabs[c], 1, 0)
            elif SUBLANES * (c + 1) - 1 <= jp:
                ranks[c] = ranks[c] + jnp.where(other > slabs[c], 1, 0)
            else:
                ge = jnp.where(other >= slabs[c], 1, 0)
                gt = jnp.where(other > slabs[c], 1, 0)
                ranks[c] = ranks[c] + jnp.where(sub > jp - SUBLANES * c, ge, gt)
    if nb > SUBLANES * n_slab:
        ranks.append(jnp.full((nb - SUBLANES * n_slab, tq), SLC_TOP_N, jnp.int32))
    rank = jnp.concatenate(ranks, axis=0)
    return jnp.where(rank < SLC_TOP_N, 0.0, MASK_VALUE)


def _nsa_attn_kernel(qt_ref, kc_ref, vct_ref, ks_ref, vst_ref, kw_ref, vwt_ref, gt_ref, ovl_ref, o_ref,
                     qa_ref, m_ref, acc_ref, *, tq, n_slc):
    i = pl.program_id(1)
    q0 = i * tq
    hpg, dh, ng = NSA_HPG, NSA_HEAD_DIM, NSA_KV_GROUPS
    st = (m_ref, acc_ref)
    heads = [(g, h) for g in range(ng) for h in range(hpg)]

    nc = kc_ref.shape[2]
    crow = lax.broadcasted_iota(jnp.int32, (nc, tq), 0)
    qcol = lax.broadcasted_iota(jnp.int32, (nc, tq), 1)
    cbias = jnp.where(crow * CMP_STRIDE + (CMP_BLOCK - 1) <= q0 + qcol, 0.0, MASK_VALUE)
    seen = jnp.where(q0 + lax.broadcasted_iota(jnp.int32, (1, tq), 1) >= CMP_BLOCK - 1, 1.0, 0.0)
    scores = [_dot(kc_ref[0, g], qt_ref[0, g * hpg + h]) for g, h in heads]
    probs = []
    for s in scores:
        s = s + cbias
        p = jnp.exp2(s - jnp.max(s, axis=0, keepdims=True))
        probs.append(p * (seen / jnp.maximum(jnp.sum(p, axis=0, keepdims=True), 1e-30)))
    o_cmp = [_dot(vct_ref[0, g], p.astype(BF16)) for (g, h), p in zip(heads, probs)]

    ovl = ovl_ref[...]
    for g in range(ng):
        ps = probs[g * hpg]
        for h in range(1, hpg):
            ps = ps + probs[g * hpg + h]
        hi = ps.astype(BF16)
        rem = ps - hi.astype(F32)
        mid = rem.astype(BF16)
        low = (rem - mid.astype(F32)).astype(BF16)
        imp = _dot(ovl, hi) + _dot(ovl, mid) + _dot(ovl, low)
        bias = _selection_bias(imp, i, tq, n_slc).astype(BF16)
        for h in range(hpg):
            qa_ref[g * hpg + h, 0:dh, :] = qt_ref[0, g * hpg + h]
            qa_ref[g * hpg + h, dh:2 * dh, :] = bias

    def kchunk(ref, g):
        return lambda c: ref[0, g, pl.ds(pl.multiple_of(c * KEY_CHUNK, KEY_CHUNK), KEY_CHUNK), :]

    nh = len(heads)
    n_back = WINDOW // KEY_CHUNK

    def win_ops():
        key = lax.broadcasted_iota(jnp.int32, (KEY_CHUNK, tq), 0)
        qry = lax.broadcasted_iota(jnp.int32, (KEY_CHUNK, tq), 1)
        ops = []
        for d in range(n_back, -1, -1):
            c = jnp.maximum(i - d, 0)
            mask = (key > qry) if d == n_back else ((key <= qry) if d == 0 else None)
            bias = jnp.where(i >= d, 0.0, MASK_VALUE) if d > 0 else None
            ops += [(nh + g * hpg + h, qt_ref[0, g * hpg + h], kchunk(kw_ref, g)(c), vwt_ref[0, g, c], mask, bias)
                    for g, h in heads]
        return ops

    _causal_sweep(st, [(lambda n=g * hpg + h: qa_ref[n], kchunk(ks_ref, g), lambda c, g=g: vst_ref[0, g, c])
                       for g, h in heads], i, tq, extra_ops_fn=win_ops)
    o_slc = [_flash_result(st, n) for n in range(nh)]
    o_win = [_flash_result(st, nh + n) for n in range(nh)]

    outs = []
    for g in range(ng):
        sg = jax.nn.sigmoid(gt_ref[0, g])
        for h in range(hpg):
            n = g * hpg + h
            outs.append(sg[3 * h:3 * h + 1] * o_cmp[n] + sg[3 * h + 1:3 * h + 2] * o_slc[n]
                        + sg[3 * h + 2:3 * h + 3] * o_win[n])
    o_ref[0] = jnp.concatenate(outs, axis=0).T.astype(o_ref.dtype)


def _nsa_attn(qt, kcmp, vcmpt, kslc, vslct, kwin, vwint, gatet, ovl, b, s):
    tq = TQ_ATTN
    assert tq == KEY_CHUNK and s % tq == 0 and WINDOW % KEY_CHUNK == 0
    n_slc = s // SLC_BLOCK
    assert n_slc <= NSA_HEAD_DIM
    g, dh = NSA_KV_GROUPS, NSA_HEAD_DIM
    nc = kcmp.shape[2]
    whole = lambda a: pl.BlockSpec((1,) + a.shape[1:], lambda bi, i: (bi,) + (0,) * (a.ndim - 1))
    return pl.pallas_call(
        functools.partial(_nsa_attn_kernel, tq=tq, n_slc=n_slc),
        out_shape=jax.ShapeDtypeStruct((b, s, NSA_HEADS * dh), BF16),
        grid=(b, s // tq),
        in_specs=[pl.BlockSpec((1, NSA_HEADS, dh, tq), lambda bi, i: (bi, 0, 0, i)),
                  whole(kcmp), whole(vcmpt), whole(kslc), whole(vslct), whole(kwin), whole(vwint),
                  pl.BlockSpec((1, g, GATE_ROWS, tq), lambda bi, i: (bi, 0, 0, i)),
                  pl.BlockSpec(ovl.shape, lambda bi, i: (0, 0))],
        out_specs=pl.BlockSpec((1, tq, NSA_HEADS * dh), lambda bi, i: (bi, i, 0)),
        scratch_shapes=[pltpu.VMEM((NSA_HEADS, 2 * dh, tq), BF16)] + _flash_scratch(2 * NSA_HEADS, tq),
        compiler_params=_params("parallel", "arbitrary"),
        name="nsa_attn",
    )(qt, kcmp, vcmpt, kslc, vslct, kwin, vwint, gatet, ovl)


def _merge_kernel(x_ref, oa_ref, cv_ref, on_ref, gm_ref, woa_ref, wcv_ref, bcv_ref, won_ref, wout_ref, o_ref):
    d = x_ref.shape[1]
    o_a = _dot(oa_ref[...], woa_ref[...])
    o_b = _dot(cv_ref[...], wcv_ref[...]) + bcv_ref[...]
    o_c = _dot(on_ref[...], won_ref[...])
    mix = (jax.nn.sigmoid(gm_ref[:, 0:d]) * o_a + jax.nn.sigmoid(gm_ref[:, d:2 * d]) * o_b
           + jax.nn.sigmoid(gm_ref[:, 2 * d:3 * d]) * o_c)
    o_ref[...] = x_ref[...] + _dot(mix.astype(BF16), wout_ref[...])


def _merge(x2d, oa, cv, on, gm, woa, wcv, bcv, won, wout):
    t, d = x2d.shape
    tm = min(TM_MERGE, t)
    row = lambda a: pl.BlockSpec((tm, a.shape[1]), lambda i: (i, 0))
    full = lambda a: pl.BlockSpec(a.shape, lambda i: (0,) * a.ndim)
    return pl.pallas_call(
        _merge_kernel,
        out_shape=jax.ShapeDtypeStruct((t, d), F32),
        grid=(t // tm,),
        in_specs=[row(x2d), row(oa), row(cv), row(on), row(gm), full(woa), full(wcv), full(bcv), full(won), full(wout)],
        out_specs=pl.BlockSpec((tm, d), lambda i: (i, 0)),
        compiler_params=_params("parallel"),
        name="merge",
    )(x2d, oa, cv, on, gm, woa, wcv, bcv, won, wout)


def _ffn_kernel(x_ref, g_ref, w1_ref, w2_ref, o_ref, h_ref, acc_ref):
    j = pl.program_id(1)

    @pl.when(j == 0)
    def _():
        x = x_ref[...]
        h_ref[...] = _rms(x, g_ref[...]).astype(h_ref.dtype)
        acc_ref[...] = x

    a = jnp.maximum(_dot(h_ref[...], w1_ref[...]), 0.0)
    acc_ref[...] += _dot((a * a).astype(BF16), w2_ref[...])

    @pl.when(j == pl.num_programs(1) - 1)
    def _():
        o_ref[...] = acc_ref[...]


def _ffn(x2d, g, w1, w2):
    t, d = x2d.shape
    f = w1.shape[1]
    tm = min(TM_FFN, t)
    tf = min(TF_FFN, f)
    return pl.pallas_call(
        _ffn_kernel,
        out_shape=jax.ShapeDtypeStruct((t, d), F32),
        grid=(t // tm, f // tf),
        in_specs=[pl.BlockSpec((tm, d), lambda i, j: (i, 0)),
                  pl.BlockSpec((1, d), lambda i, j: (0, 0)),
                  pl.BlockSpec((d, tf), lambda i, j: (0, j)),
                  pl.BlockSpec((tf, d), lambda i, j: (j, 0))],
        out_specs=pl.BlockSpec((tm, d), lambda i, j: (i, 0)),
        scratch_shapes=[pltpu.VMEM((tm, d), BF16), pltpu.VMEM((tm, d), F32)],
        compiler_params=_params("parallel", "arbitrary"),
        name="ffn",
    )(x2d, g, w1, w2)


def _rope_tables(pos, rot_dim, group, width=LANES):
    half = rot_dim // 2
    inv = jnp.power(jnp.float32(ROPE_THETA), -jnp.arange(half, dtype=jnp.float32) * (2.0 / rot_dim))
    ang = pos.astype(jnp.float32)[:, None] * inv
    n = ang.shape[0]
    cos = jnp.concatenate([jnp.cos(ang), jnp.cos(ang), jnp.ones((n, group - rot_dim), F32)], axis=1)
    sin = jnp.concatenate([-jnp.sin(ang), jnp.sin(ang), jnp.zeros((n, group - rot_dim), F32)], axis=1)
    reps = width // group
    return jnp.tile(cos, (1, reps)), jnp.tile(sin, (1, reps))


def _pad_cols(w, width):
    return jnp.pad(w, ((0, 0), (0, width - w.shape[1])))


def _pad_last(w, width):
    return jnp.pad(w, [(0, 0)] * (w.ndim - 1) + [(0, width - w.shape[-1])])


def _rope_partner(w):
    half = w.shape[-1] // 2
    return jnp.concatenate([w[..., half:], w[..., :half]], axis=-1)


def _overlap_t(s):
    n_cmp_pad = s // CMP_STRIDE
    n_slc = s // SLC_BLOCK
    c_start = np.arange(n_cmp_pad) * CMP_STRIDE
    s_start = np.arange(n_slc) * SLC_BLOCK
    ov = ((c_start[None, :] < s_start[:, None] + SLC_BLOCK) & (c_start[None, :] + CMP_BLOCK > s_start[:, None]))
    out = np.zeros((NSA_HEAD_DIM, n_cmp_pad), np.float32)
    out[:n_slc] = ov
    return jnp.asarray(out, BF16)


def _compress_weights(w1, w2):
    g, dh, hid = NSA_KV_GROUPS, NSA_HEAD_DIM, CMP_HIDDEN
    w1r = w1.reshape(2, CMP_STRIDE, dh, hid)
    wide = jnp.zeros((2, CMP_STRIDE, g, dh, g, hid), w1.dtype)
    for gi in range(g):
        wide = wide.at[:, :, gi, :, gi, :].set(w1r)
    wide = wide.reshape(2, CMP_STRIDE * g * dh, g * hid).astype(BF16)
    w2w = jnp.zeros((g, hid, g, dh), w2.dtype)
    for gi in range(g):
        w2w = w2w.at[gi, :, gi, :].set(w2)
    return wide[0], wide[1], w2w.reshape(g * hid, g * dh).astype(BF16)


def _compress_pe(pe):
    pe2 = pe.reshape(2, CMP_STRIDE, 1, NSA_HEAD_DIM)
    return jnp.broadcast_to(pe2, (2, CMP_STRIDE, NSA_KV_GROUPS, NSA_HEAD_DIM)).reshape(2, -1).astype(F32)


def kernel(x, g_mix, w_in, g_cq, g_ckv, w_uq, w_ukv, g_q_mla, g_k_mla, w_o_mla, b_glu, w_dw, b_dw, g_conv_ln, b_conv_ln, w_conv_out, b_conv_out, pe_cmp_k, pe_cmp_v, w_cmp_k1, w_cmp_k2, w_cmp_v1, w_cmp_v2, g_q_nsa, g_k_nsa, w_o_nsa, w_out, g_ffn, w_ff1, w_ff2):
    b, s, d = x.shape
    depth = w_in.shape[0]
    t = b * s
    q_rank, kv_rank = g_cq.shape[1], g_ckv.shape[1]
    nh, hd = NSA_HEADS, NSA_HEAD_DIM
    ngd = NSA_KV_GROUPS * hd
    assert ngd == LANES and s % CMP_STRIDE == 0

    pos = jnp.arange(s)
    cos_mla, sin_mla = _rope_tables(pos, MLA_ROPE_DIM, LANES)
    cos_nsa, sin_nsa = _rope_tables(pos, NSA_ROPE_DIM, hd)
    cmp_pos = jnp.arange(s // CMP_STRIDE) * CMP_STRIDE + CMP_BLOCK - 1
    cos_cmp, sin_cmp = _rope_tables(cmp_pos, NSA_ROPE_DIM, hd)
    ovl = _overlap_t(s)

    splits = (q_rank, kv_rank, MLA_ROPE_DIM, 2 * CONV_CH, nh * hd, 6 * ngd, 3 * nh, 3 * d)
    offs = np.concatenate([[0], np.cumsum(splits)])
    widths = (q_rank, kv_rank, LANES, LANES, 2 * CONV_CH, nh * hd) + (LANES,) * 6 + (NSA_KV_GROUPS * LANES, 3 * d)

    x2d = x.reshape(t, d)
    for l in range(depth):
        wl = w_in[l]
        seg = [wl[:, offs[k]:offs[k + 1]] for k in range(len(splits))]
        gate_cols = [_pad_cols(seg[6][:, gi * NSA_HPG * 3:(gi + 1) * NSA_HPG * 3], LANES) for gi in range(NSA_KV_GROUPS)]
        w_in_p = jnp.concatenate(
            [seg[0], seg[1], _pad_cols(seg[2], LANES), _pad_cols(_rope_partner(seg[2]), LANES), seg[3], seg[4], seg[5]]
            + gate_cols + [seg[7]], axis=1).astype(BF16)
        (cq, ckv, kr, krot, u2, qn, kc, vc, ks, vs, kw, vw, gn, gm) = _inproj(x2d, g_mix[l][None, :], w_in_p, widths)

        wuq = w_uq[l].reshape(q_rank, MLA_HEADS, MLA_QK_DIM)
        wuq = jnp.concatenate([_pad_last(wuq, LANES), _pad_last(_rope_partner(wuq[..., :MLA_ROPE_DIM]), LANES)], axis=1)
        wuq = wuq.reshape(q_rank, 2 * MLA_HEADS * LANES).astype(BF16)
        wukv = w_ukv[l].reshape(kv_rank, MLA_HEADS, MLA_NOPE_DIM + V_DIM)
        wuk = jnp.pad(wukv[:, :, :MLA_NOPE_DIM], ((0, 0), (0, 0), (MLA_ROPE_DIM, LANES - MLA_QK_DIM)))
        wuk = wuk.reshape(kv_rank, MLA_HEADS * LANES).astype(BF16)
        wuv = wukv[:, :, MLA_NOPE_DIM:].reshape(kv_rank, MLA_HEADS * V_DIM).astype(BF16)
        gq = g_q_mla[l] * (MLA_QK_DIM ** -0.5 * LOG2_E)
        gq = jnp.stack([_pad_last(gq, LANES), _pad_last(_rope_partner(gq[:MLA_ROPE_DIM]), LANES)])
        gk = jnp.stack([_pad_last(g_k_mla[l], LANES), _pad_last(_rope_partner(g_k_mla[l][:MLA_ROPE_DIM]), LANES)])
        qt_m, k_m, vt_m = _mla_prep(cq, ckv, kr, krot, cos_mla, sin_mla, g_cq[l][None, :], g_ckv[l][None, :],
                                    wuq, wuk, wuv, gq, gk, b, s)
        o_mla = _mla_attn(qt_m, k_m, vt_m).reshape(t, MLA_HEADS * V_DIM)

        wdw = jnp.pad(w_dw[l], ((0, CONV_HALO - CONV_WIDTH), (0, 0)))
        o_cv = _conv(u2, b_glu[l][None, :], wdw, b_dw[l][None, :], g_conv_ln[l][None, :], b_conv_ln[l][None, :],
                     b, s).reshape(t, CONV_CH)

        gqn = jnp.tile(g_q_nsa[l] * (hd ** -0.5 * LOG2_E), 2)[None, :]
        gkn = jnp.tile(g_k_nsa[l], 2)[None, :]
        qt_n, k_slc, vt_slc, k_win, vt_win, gate_t = _nsa_prep(qn, ks, vs, kw, vw, gn, cos_nsa, sin_nsa, gqn, gkn, b, s)
        wka, wkb, wk2 = _compress_weights(w_cmp_k1[l], w_cmp_k2[l])
        wva, wvb, wv2 = _compress_weights(w_cmp_v1[l], w_cmp_v2[l])
        k_cmp, vt_cmp = _compress(kc, vc, _compress_pe(pe_cmp_k[l]), _compress_pe(pe_cmp_v[l]),
                                  wka, wkb, wk2, wva, wvb, wv2, gkn, cos_cmp, sin_cmp, b, s)
        o_nsa = _nsa_attn(qt_n, k_cmp, vt_cmp, k_slc, vt_slc, k_win, vt_win, gate_t, ovl, b, s).reshape(t, nh * hd)

        x2d = _merge(x2d, o_mla, o_cv, o_nsa, gm, w_o_mla[l].astype(BF16), w_conv_out[l].astype(BF16),
                     b_conv_out[l][None, :], w_o_nsa[l].astype(BF16), w_out[l].astype(BF16))
        x2d = _ffn(x2d, g_ffn[l][None, :], w1=w_ff1[l].astype(BF16), w2=w_ff2[l].astype(BF16))
    return x2d.reshape(b, s, d)
```

```python
import functools
import math

import numpy as np
import jax
import jax.numpy as jnp
from jax import lax
from jax.experimental import pallas as pl
from jax.experimental.pallas import tpu as pltpu

F32 = jnp.float32
BF16 = jnp.bfloat16

ROPE_THETA = 500000.0
NORM_EPS = 1e-6
MLA_HEADS = 8
MLA_NOPE_DIM = 64
MLA_ROPE_DIM = 32
MLA_QK_DIM = MLA_NOPE_DIM + MLA_ROPE_DIM
CONV_CH = 512
CONV_WIDTH = 31
NSA_HEADS = 8
NSA_KV_GROUPS = 2
NSA_HEAD_DIM = 64
NSA_ROPE_DIM = NSA_HEAD_DIM // 4
NSA_HPG = NSA_HEADS // NSA_KV_GROUPS
CMP_BLOCK = 32
CMP_STRIDE = 16
CMP_HIDDEN = 128
SLC_BLOCK = 64
SLC_TOP_N = 16
WINDOW = 512
FORCED_BLOCK_SCORE = 1e6
V_DIM = 64

LANES = 128
SUBLANES = 8
BF16_SUBLANES = 16
VMEM_LIMIT_BYTES = 56 * 1024 * 1024

MASK_VALUE = -1e30
LOG2_E = math.log2(math.e)
V_AUG = V_DIM + BF16_SUBLANES

TM_PROJ = 256
TM_PREP = 512
TQ_ATTN = 256
KEY_CHUNK = 256
SCORE_LOOKAHEAD = 6
TS_CONV = 256
CONV_HALO = 32
CONV_CHUNK = 64
TM_MERGE = 512
TM_FFN = 1024
TF_FFN = 512


def _dot(a, b):
    return jnp.dot(a, b, preferred_element_type=F32)


def _rms(x, g):
    return x * lax.rsqrt(jnp.mean(x * x, axis=-1, keepdims=True) + NORM_EPS) * g


def _params(*sem):
    return pltpu.CompilerParams(dimension_semantics=sem, vmem_limit_bytes=VMEM_LIMIT_BYTES)


def _flash_scratch(n_chains, n):
    return [pltpu.VMEM((n_chains, 1, n), F32), pltpu.VMEM((n_chains, V_AUG, n), F32)]


def _flash_reset(st):
    m_ref, acc_ref = st
    m_ref[...] = jnp.full(m_ref.shape, MASK_VALUE, F32)
    acc_ref[...] = jnp.zeros(acc_ref.shape, F32)


def _flash_steps(st, ops, lookahead=None):
    m_ref, acc_ref = st

    def softmax(j, s):
        n, _, _, _, mask, bias = ops[j]
        if mask is not None:
            s = jnp.where(mask, s, MASK_VALUE)
        if bias is not None:
            s = s + bias
        m = m_ref[n]
        m_new = jnp.maximum(m, jnp.max(s, axis=0, keepdims=True))
        m_ref[n] = m_new
        return jnp.exp2(m - m_new), jnp.exp2(s - m_new).astype(BF16)

    if lookahead is None:
        scores = [_dot(op[2], op[1]) for op in ops]
        probs = [softmax(j, s) for j, s in enumerate(scores)]
        pending = [(op[0], alpha, _dot(op[3], p)) for op, (alpha, p) in zip(ops, probs)]
    else:
        scores = {}
        pending = []
        for t in range(len(ops) + lookahead):
            if t < len(ops):
                scores[t] = _dot(ops[t][2], ops[t][1])
            j = t - lookahead
            if j >= 0:
                alpha, p = softmax(j, scores.pop(j))
                pending.append((ops[j][0], alpha, _dot(ops[j][3], p)))
    for n, alpha, pv in pending:
        acc_ref[n] = alpha * acc_ref[n] + pv


def _flash_result(st, n):
    _, acc_ref = st
    return acc_ref[n, 0:V_DIM, :] / acc_ref[n, V_DIM:V_DIM + 1, :]


def _causal_sweep(st, chains, i, tq, extra_ops_fn=None):
    _flash_reset(st)

    def ops_for(c, mask):
        return [(n, qt_fn(), k_fn(c), vt_fn(c), mask, None) for n, (qt_fn, k_fn, vt_fn) in enumerate(chains)]

    def pair(j, carry):
        _flash_steps(st, ops_for(2 * j, None) + ops_for(2 * j + 1, None), SCORE_LOOKAHEAD)
        return carry

    lax.fori_loop(0, i // 2, pair, 0)
    key = lax.broadcasted_iota(jnp.int32, (KEY_CHUNK, tq), 0)
    qry = lax.broadcasted_iota(jnp.int32, (KEY_CHUNK, tq), 1)
    extra = extra_ops_fn if extra_ops_fn is not None else (lambda: [])

    @pl.when(i % 2 == 1)
    def _():
        _flash_steps(st, ops_for(i - 1, None) + ops_for(i, key <= qry) + extra())

    @pl.when(i % 2 == 0)
    def _():
        _flash_steps(st, ops_for(i, key <= qry) + extra())


def _ones_rows(n):
    return jnp.where(lax.broadcasted_iota(jnp.int32, (V_AUG - V_DIM, n), 0) == 0, 1.0, 0.0).astype(BF16)


def _store_vt(v, vt_ref, n_heads):
    ones = _ones_rows(KEY_CHUNK)
    for p in range(n_heads // 2):
        t = v[:, p * LANES:(p + 1) * LANES].T.astype(vt_ref.dtype)
        for c in range(vt_ref.shape[2]):
            for e in range(2):
                vt_ref[0, 2 * p + e, c, 0:V_DIM, :] = t[e * V_DIM:(e + 1) * V_DIM, c * KEY_CHUNK:(c + 1) * KEY_CHUNK]
                vt_ref[0, 2 * p + e, c, V_DIM:V_AUG, :] = ones


def _vt_shape(b, n, s):
    return jax.ShapeDtypeStruct((b, n, s // KEY_CHUNK, V_AUG, KEY_CHUNK), BF16)


def _inproj_kernel(x_ref, g_ref, w_ref, *out_refs, widths):
    h = _rms(x_ref[...], g_ref[...]).astype(BF16)
    off = 0
    for o_ref, wd in zip(out_refs, widths):
        o_ref[...] = _dot(h, w_ref[:, off:off + wd]).astype(o_ref.dtype)
        off += wd


def _inproj(x2d, g, w, widths, dtypes):
    t, d = x2d.shape
    n = w.shape[1]
    return pl.pallas_call(
        functools.partial(_inproj_kernel, widths=widths),
        out_shape=[jax.ShapeDtypeStruct((t, wd), dt) for wd, dt in zip(widths, dtypes)],
        grid=(t // TM_PROJ,),
        in_specs=[pl.BlockSpec((TM_PROJ, d), lambda i: (i, 0)),
                  pl.BlockSpec((1, d), lambda i: (0, 0)),
                  pl.BlockSpec((d, n), lambda i: (0, 0), pipeline_mode=pl.Buffered(1))],
        out_specs=[pl.BlockSpec((TM_PROJ, wd), lambda i: (i, 0)) for wd in widths],
        compiler_params=_params("parallel"),
        name="inproj",
    )(x2d, g, w)


def _mla_prep_kernel(cq_ref, ckv_ref, kr_ref, krot_ref, cos_ref, sin_ref, gcq_ref, gckv_ref,
                     wuq_ref, wuk_ref, wuv_ref, gq_ref, gk_ref, qt_ref, k_ref, vt_ref):
    cos = cos_ref[...]
    sin = sin_ref[...]
    q_cos, q_sin = gq_ref[0:1, :] * cos, gq_ref[1:2, :] * sin
    k_cos, k_sin = gk_ref[0:1, :] * cos, gk_ref[1:2, :] * sin
    inv_dim = 1.0 / MLA_QK_DIM
    hw = MLA_HEADS * LANES
    qq = _dot(_rms(cq_ref[...], gcq_ref[...]).astype(BF16), wuq_ref[...])
    for h in range(MLA_HEADS):
        qh = qq[:, h * LANES:(h + 1) * LANES]
        qp = qq[:, hw + h * LANES:hw + (h + 1) * LANES]
        r = lax.rsqrt(jnp.sum(qh * qh, axis=-1, keepdims=True) * inv_dim + NORM_EPS)
        qt_ref[0, h] = ((qh * q_cos + qp * q_sin) * r).T.astype(qt_ref.dtype)

    ckvn = _rms(ckv_ref[...], gckv_ref[...]).astype(BF16)
    kn = _dot(ckvn, wuk_ref[...])
    v = _dot(ckvn, wuv_ref[...])
    kr = kr_ref[...]
    kr_rot = krot_ref[...] * k_sin
    for h in range(MLA_HEADS):
        kh = kn[:, h * LANES:(h + 1) * LANES] + kr
        r = lax.rsqrt(jnp.sum(kh * kh, axis=-1, keepdims=True) * inv_dim + NORM_EPS)
        k_ref[0, h] = ((kh * k_cos + kr_rot) * r).astype(k_ref.dtype)
    _store_vt(v, vt_ref, MLA_HEADS)


def _mla_prep(cq, ckv, kr, krot, cos, sin, gcq, gckv, wuq, wuk, wuv, gq, gk, b, s):
    tm = min(TM_PREP, s)
    nb = s // tm
    row = lambda width: pl.BlockSpec((tm, width), lambda bi, i: (bi * nb + i, 0))
    tab = pl.BlockSpec((tm, LANES), lambda bi, i: (i, 0))
    full = lambda a: pl.BlockSpec(a.shape, lambda bi, i: (0,) * a.ndim)
    return pl.pallas_call(
        _mla_prep_kernel,
        out_shape=[jax.ShapeDtypeStruct((b, MLA_HEADS, LANES, s), BF16),
                   jax.ShapeDtypeStruct((b, MLA_HEADS, s, LANES), BF16),
                   _vt_shape(b, MLA_HEADS, s)],
        grid=(b, nb),
        in_specs=[row(cq.shape[1]), row(ckv.shape[1]), row(LANES), row(LANES), tab, tab,
                  full(gcq), full(gckv), full(wuq), full(wuk), full(wuv), full(gq), full(gk)],
        out_specs=[pl.BlockSpec((1, MLA_HEADS, LANES, tm), lambda bi, i: (bi, 0, 0, i)),
                   pl.BlockSpec((1, MLA_HEADS, tm, LANES), lambda bi, i: (bi, 0, i, 0)),
                   pl.BlockSpec((1, MLA_HEADS, tm // KEY_CHUNK, V_AUG, KEY_CHUNK), lambda bi, i: (bi, 0, i, 0, 0))],
        compiler_params=_params("parallel", "parallel"),
        name="mla_prep",
    )(cq, ckv, kr, krot, cos, sin, gcq, gckv, wuq, wuk, wuv, gq, gk)


def _mla_attn_kernel(qt_ref, k_ref, vt_ref, o_ref, m_ref, acc_ref, *, tq):
    i = pl.program_id(1)
    st = (m_ref, acc_ref)
    chains = [(lambda h=h: qt_ref[0, h],
               lambda c, h=h: k_ref[0, h, pl.ds(pl.multiple_of(c * KEY_CHUNK, KEY_CHUNK), KEY_CHUNK), :],
               lambda c, h=h: vt_ref[0, h, c]) for h in range(MLA_HEADS)]
    _causal_sweep(st, chains, i, tq)
    outs = [_flash_result(st, h) for h in range(MLA_HEADS)]
    o_ref[0] = jnp.concatenate(outs, axis=0).T.astype(o_ref.dtype)


def _mla_attn(qt, k, vt):
    b, nh, s, _ = k.shape
    tq = TQ_ATTN
    assert tq == KEY_CHUNK and s % tq == 0
    return pl.pallas_call(
        functools.partial(_mla_attn_kernel, tq=tq),
        out_shape=jax.ShapeDtypeStruct((b, s, nh * V_DIM), BF16),
        grid=(b, s // tq),
        in_specs=[pl.BlockSpec((1, nh, LANES, tq), lambda bi, i: (bi, 0, 0, i)),
                  pl.BlockSpec((1, nh, s, LANES), lambda bi, i: (bi, 0, 0, 0)),
                  pl.BlockSpec((1, nh, s // KEY_CHUNK, V_AUG, KEY_CHUNK), lambda bi, i: (bi, 0, 0, 0, 0))],
        out_specs=pl.BlockSpec((1, tq, nh * V_DIM), lambda bi, i: (bi, i, 0)),
        scratch_shapes=_flash_scratch(nh, tq),
        compiler_params=_params("parallel", "arbitrary"),
        name="mla_attn",
    )(qt, k, vt)


def _conv_kernel(u_ref, halo_ref, bglu_ref, wdw_ref, bdw_ref, gln_ref, bln_ref, o_ref, ext_ref, *, ts):
    i = pl.program_id(1)
    bglu = bglu_ref[...]

    def glu(u2):
        u2 = u2 + bglu
        return u2[:, :CONV_CH] * jax.nn.sigmoid(u2[:, CONV_CH:])

    ext_ref[0, 0:CONV_HALO, :] = jnp.where(i > 0, glu(halo_ref[0]), 0.0)
    ext_ref[0, CONV_HALO:, :] = glu(u_ref[0])
    rows = ts + CONV_HALO - SUBLANES
    for b in range(1, SUBLANES):
        ext_ref[b, 0:rows, :] = ext_ref[0, b:b + rows, :]
    wdw = wdw_ref[...]
    shift = CONV_HALO - (CONV_WIDTH - 1)
    for r0 in range(0, ts, CONV_CHUNK):
        acc = jnp.zeros((CONV_CHUNK, CONV_CH), F32)
        for k in range(CONV_WIDTH):
            a, b = divmod(shift + k, SUBLANES)
            acc = acc + wdw[k:k + 1, :] * ext_ref[b, r0 + SUBLANES * a:r0 + SUBLANES * a + CONV_CHUNK, :]
        acc = acc + bdw_ref[...]
        mu = jnp.mean(acc, axis=-1, keepdims=True)
        d = acc - mu
        var = jnp.mean(d * d, axis=-1, keepdims=True)
        y = d * lax.rsqrt(var + NORM_EPS) * gln_ref[...] + bln_ref[...]
        o_ref[0, r0:r0 + CONV_CHUNK, :] = (y * jax.nn.sigmoid(y)).astype(o_ref.dtype)


def _conv(u2, bglu, wdw, bdw, gln, bln, b, s):
    ts = min(TS_CONV, s)
    u3 = u2.reshape(b, s, 2 * CONV_CH)
    hb = ts // CONV_HALO
    full = lambda a: pl.BlockSpec(a.shape, lambda bi, i: (0,) * a.ndim)
    return pl.pallas_call(
        functools.partial(_conv_kernel, ts=ts),
        out_shape=jax.ShapeDtypeStruct((b, s, CONV_CH), BF16),
        grid=(b, s // ts),
        in_specs=[pl.BlockSpec((1, ts, 2 * CONV_CH), lambda bi, i: (bi, i, 0)),
                  pl.BlockSpec((1, CONV_HALO, 2 * CONV_CH), lambda bi, i: (bi, jnp.maximum(i * hb - 1, 0), 0)),
                  full(bglu), full(wdw), full(bdw), full(gln), full(bln)],
        out_specs=pl.BlockSpec((1, ts, CONV_CH), lambda bi, i: (bi, i, 0)),
        scratch_shapes=[pltpu.VMEM((SUBLANES, ts + CONV_HALO, CONV_CH), F32)],
        compiler_params=_params("parallel", "parallel"),
        name="conv_mixer",
    )(u3, u3, bglu, wdw, bdw, gln, bln)


def _pair_norm_rope(x, g, cos, sin):
    lane = lax.broadcasted_iota(jnp.int32, x.shape, 1)
    lo = lane < NSA_HEAD_DIM
    x2 = x * x
    s_lo = jnp.sum(jnp.where(lo, x2, 0.0), axis=-1, keepdims=True)
    s_hi = jnp.sum(jnp.where(lo, 0.0, x2), axis=-1, keepdims=True)
    ss = jnp.where(lo, s_lo, s_hi) * (1.0 / NSA_HEAD_DIM)
    y = x * lax.rsqrt(ss + NORM_EPS) * g
    half = NSA_ROPE_DIM // 2
    first = (lane & (NSA_HEAD_DIM - 1)) < half
    rot = jnp.where(first, pltpu.roll(y, LANES - half, 1), pltpu.roll(y, half, 1))
    return y * cos + rot * sin


def _nsa_prep_kernel(qn_ref, ks_ref, vs_ref, kw_ref, vw_ref, gate_ref, cos_ref, sin_ref, gq_ref, gk_ref,
                     qt_ref, kslc_ref, vslct_ref, kwin_ref, vwint_ref, gatet_ref, *, tm):
    i = pl.program_id(1)
    cos = cos_ref[...]
    sin = sin_ref[...]
    dh = NSA_HEAD_DIM
    for p in range(NSA_HEADS // 2):
        yt = _pair_norm_rope(qn_ref[:, p * LANES:(p + 1) * LANES], gq_ref[...], cos, sin).T.astype(qt_ref.dtype)
        qt_ref[0, 2 * p] = yt[0:dh]
        qt_ref[0, 2 * p + 1] = yt[dh:2 * dh]
    ks = _pair_norm_rope(ks_ref[...], gk_ref[...], cos, sin).astype(kslc_ref.dtype)
    kw = _pair_norm_rope(kw_ref[...], gk_ref[...], cos, sin).astype(kwin_ref.dtype)
    blk = (i * tm + lax.broadcasted_iota(jnp.int32, (tm, dh), 0)) // SLC_BLOCK
    onehot = jnp.where(lax.broadcasted_iota(jnp.int32, (tm, dh), 1) == blk, 1.0, 0.0).astype(kslc_ref.dtype)
    for g in range(NSA_KV_GROUPS):
        kslc_ref[0, g, :, 0:dh] = ks[:, g * dh:(g + 1) * dh]
        kslc_ref[0, g, :, dh:2 * dh] = onehot
        kwin_ref[0, g] = kw[:, g * dh:(g + 1) * dh]
    _store_vt(vs_ref[...], vslct_ref, NSA_KV_GROUPS)
    _store_vt(vw_ref[...], vwint_ref, NSA_KV_GROUPS)
    for g in range(NSA_KV_GROUPS):
        gatet_ref[0, g] = gate_ref[:, g * LANES:(g + 1) * LANES].T[0:GATE_ROWS]


GATE_ROWS = 16


def _nsa_prep(qn, ks, vs, kw, vw, gates, cos, sin, gq, gk, b, s):
    tm = min(TM_PREP, s)
    nb = s // tm
    row = lambda width: pl.BlockSpec((tm, width), lambda bi, i: (bi * nb + i, 0))
    tab = pl.BlockSpec((tm, LANES), lambda bi, i: (i, 0))
    full = lambda a: pl.BlockSpec(a.shape, lambda bi, i: (0,) * a.ndim)
    g, dh = NSA_KV_GROUPS, NSA_HEAD_DIM
    vt_spec = pl.BlockSpec((1, g, tm // KEY_CHUNK, V_AUG, KEY_CHUNK), lambda bi, i: (bi, 0, i, 0, 0))
    return pl.pallas_call(
        functools.partial(_nsa_prep_kernel, tm=tm),
        out_shape=[jax.ShapeDtypeStruct((b, NSA_HEADS, dh, s), BF16),
                   jax.ShapeDtypeStruct((b, g, s, 2 * dh), BF16),
                   _vt_shape(b, g, s),
                   jax.ShapeDtypeStruct((b, g, s, dh), BF16),
                   _vt_shape(b, g, s),
                   jax.ShapeDtypeStruct((b, g, GATE_ROWS, s), F32)],
        grid=(b, nb),
        in_specs=[row(qn.shape[1]), row(LANES), row(LANES), row(LANES), row(LANES), row(g * LANES),
                  tab, tab, full(gq), full(gk)],
        out_specs=[pl.BlockSpec((1, NSA_HEADS, dh, tm), lambda bi, i: (bi, 0, 0, i)),
                   pl.BlockSpec((1, g, tm, 2 * dh), lambda bi, i: (bi, 0, i, 0)),
                   vt_spec,
                   pl.BlockSpec((1, g, tm, dh), lambda bi, i: (bi, 0, i, 0)),
                   vt_spec,
                   pl.BlockSpec((1, g, GATE_ROWS, tm), lambda bi, i: (bi, 0, 0, i))],
        compiler_params=_params("parallel", "parallel"),
        name="nsa_prep",
    )(qn, ks, vs, kw, vw, gates, cos, sin, gq, gk)


def _compress_kernel(kc_ref, vc_ref, pek_ref, pev_ref, wka_ref, wkb_ref, wk2_ref, wva_ref, wvb_ref, wv2_ref,
                     gk_ref, cos_ref, sin_ref, kcmp_ref, vcmpt_ref):
    def comp(r_ref, pe_ref, wa_ref, wb_ref, w2_ref):
        r = r_ref[0]
        nr = r.shape[0]
        a = _dot((r + pe_ref[0:1, :]).astype(BF16), wa_ref[...])
        bb = _dot((r + pe_ref[1:2, :]).astype(BF16), wb_ref[...])
        hd = a + pltpu.roll(bb, nr - 1, 0)
        hd = hd * jax.nn.sigmoid(hd)
        return _dot(hd.astype(BF16), w2_ref[...])

    k = _pair_norm_rope(comp(kc_ref, pek_ref, wka_ref, wkb_ref, wk2_ref), gk_ref[...], cos_ref[...], sin_ref[...])
    vt = comp(vc_ref, pev_ref, wva_ref, wvb_ref, wv2_ref).T
    dh = NSA_HEAD_DIM
    for g in range(NSA_KV_GROUPS):
        kcmp_ref[0, g] = k[:, g * dh:(g + 1) * dh].astype(kcmp_ref.dtype)
        vcmpt_ref[0, g] = vt[g * dh:(g + 1) * dh].astype(vcmpt_ref.dtype)


def _compress(kc, vc, pek, pev, wka, wkb, wk2, wva, wvb, wv2, gk, cos, sin, b, s):
    nr = s // CMP_STRIDE
    width = CMP_STRIDE * LANES
    kc3 = kc.reshape(b, nr, width)
    vc3 = vc.reshape(b, nr, width)
    full = lambda a: pl.BlockSpec(a.shape, lambda bi: (0,) * a.ndim)
    blk = pl.BlockSpec((1, nr, width), lambda bi: (bi, 0, 0))
    g, dh = NSA_KV_GROUPS, NSA_HEAD_DIM
    return pl.pallas_call(
        _compress_kernel,
        out_shape=[jax.ShapeDtypeStruct((b, g, nr, dh), BF16), jax.ShapeDtypeStruct((b, g, dh, nr), BF16)],
        grid=(b,),
        in_specs=[blk, blk, full(pek), full(pev), full(wka), full(wkb), full(wk2), full(wva), full(wvb), full(wv2),
                  full(gk), full(cos), full(sin)],
        out_specs=[pl.BlockSpec((1, g, nr, dh), lambda bi: (bi, 0, 0, 0)),
                   pl.BlockSpec((1, g, dh, nr), lambda bi: (bi, 0, 0, 0))],
        compiler_params=_params("parallel"),
        name="nsa_compress",
    )(kc3, vc3, pek, pev, wka, wkb, wk2, wva, wvb, wv2, gk, cos, sin)


def _selection_bias(imp, i, tq, n_slc):
    nb = imp.shape[0]
    jrow = lax.broadcasted_iota(jnp.int32, (nb, tq), 0)
    blk_t = (i * tq + lax.broadcasted_iota(jnp.int32, (nb, tq), 1)) // SLC_BLOCK
    forced = (jrow == 0) | (jrow == blk_t) | (jrow == blk_t - 1)
    imp = jnp.where(forced, FORCED_BLOCK_SCORE, imp)
    imp = jnp.where(jrow <= blk_t, imp, -1.0)
    n_slab = -(-n_slc // SUBLANES)
    slabs = [imp[SUBLANES * c:SUBLANES * (c + 1), :] for c in range(n_slab)]
    ranks = [jnp.zeros((SUBLANES, tq), jnp.int32) for _ in range(n_slab)]
    sub = lax.broadcasted_iota(jnp.int32, (SUBLANES, tq), 0)
    for jp in range(n_slc):
        other = imp[jp:jp + 1, :]
        for c in range(n_slab):
            if SUBLANES * c > jp:
                ranks[c] = ranks[c] + jnp.where(other >= slabs[c], 1, 0)
            elif SUBLANES * (c + 1) - 1 <= jp:
                ranks[c] = ranks[c] + jnp.where(other > slabs[c], 1, 0)
            else:
                ge = jnp.where(other >= slabs[c], 1, 0)
                gt = jnp.where(other > slabs[c], 1, 0)
                ranks[c] = ranks[c] + jnp.where(sub > jp - SUBLANES * c, ge, gt)
    if nb > SUBLANES * n_slab:
        ranks.append(jnp.full((nb - SUBLANES * n_slab, tq), SLC_TOP_N, jnp.int32))
    rank = jnp.concatenate(ranks, axis=0)
    return jnp.where(rank < SLC_TOP_N, 0.0, MASK_VALUE)


def _nsa_attn_kernel(qt_ref, kc_ref, vct_ref, ks_ref, vst_ref, kw_ref, vwt_ref, gt_ref, ovl_ref, o_ref,
                     qa_ref, m_ref, acc_ref, *, tq, n_slc):
    i = pl.program_id(1)
    q0 = i * tq
    hpg, dh, ng = NSA_HPG, NSA_HEAD_DIM, NSA_KV_GROUPS
    st = (m_ref, acc_ref)
    heads = [(g, h) for g in range(ng) for h in range(hpg)]

    nc = kc_ref.shape[2]
    crow = lax.broadcasted_iota(jnp.int32, (nc, tq), 0)
    qcol = lax.broadcasted_iota(jnp.int32, (nc, tq), 1)
    cbias = jnp.where(crow * CMP_STRIDE + (CMP_BLOCK - 1) <= q0 + qcol, 0.0, MASK_VALUE)
    seen = jnp.where(q0 + lax.broadcasted_iota(jnp.int32, (1, tq), 1) >= CMP_BLOCK - 1, 1.0, 0.0)
    scores = [_dot(kc_ref[0, g], qt_ref[0, g * hpg + h]) for g, h in heads]
    probs = []
    for s in scores:
        s = s + cbias
        p = jnp.exp2(s - jnp.max(s, axis=0, keepdims=True))
        probs.append(p * (seen / jnp.maximum(jnp.sum(p, axis=0, keepdims=True), 1e-30)))
    o_cmp = [_dot(vct_ref[0, g], p.astype(BF16)) for (g, h), p in zip(heads, probs)]

    ovl = ovl_ref[...]
    for g in range(ng):
        ps = probs[g * hpg]
        for h in range(1, hpg):
            ps = ps + probs[g * hpg + h]
        hi = ps.astype(BF16)
        rem = ps - hi.astype(F32)
        mid = rem.astype(BF16)
        low = (rem - mid.astype(F32)).astype(BF16)
        imp = _dot(ovl, hi) + _dot(ovl, mid) + _dot(ovl, low)
        bias = _selection_bias(imp, i, tq, n_slc).astype(BF16)
        for h in range(hpg):
            qa_ref[g * hpg + h, 0:dh, :] = qt_ref[0, g * hpg + h]
            qa_ref[g * hpg + h, dh:2 * dh, :] = bias

    def kchunk(ref, g):
        return lambda c: ref[0, g, pl.ds(pl.multiple_of(c * KEY_CHUNK, KEY_CHUNK), KEY_CHUNK), :]

    nh = len(heads)
    n_back = WINDOW // KEY_CHUNK

    def win_ops():
        key = lax.broadcasted_iota(jnp.int32, (KEY_CHUNK, tq), 0)
        qry = lax.broadcasted_iota(jnp.int32, (KEY_CHUNK, tq), 1)
        ops = []
        for d in range(n_back, -1, -1):
            c = jnp.maximum(i - d, 0)
            mask = (key > qry) if d == n_back else ((key <= qry) if d == 0 else None)
            bias = jnp.where(i >= d, 0.0, MASK_VALUE) if d > 0 else None
            ops += [(nh + g * hpg + h, qt_ref[0, g * hpg + h], kchunk(kw_ref, g)(c), vwt_ref[0, g, c], mask, bias)
                    for g, h in heads]
        return ops

    _causal_sweep(st, [(lambda n=g * hpg + h: qa_ref[n], kchunk(ks_ref, g), lambda c, g=g: vst_ref[0, g, c])
                       for g, h in heads], i, tq, extra_ops_fn=win_ops)
    o_slc = [_flash_result(st, n) for n in range(nh)]
    o_win = [_flash_result(st, nh + n) for n in range(nh)]

    outs = []
    for g in range(ng):
        sg = jax.nn.sigmoid(gt_ref[0, g])
        for h in range(hpg):
            n = g * hpg + h
            outs.append(sg[3 * h:3 * h + 1] * o_cmp[n] + sg[3 * h + 1:3 * h + 2] * o_slc[n]
                        + sg[3 * h + 2:3 * h + 3] * o_win[n])
    o_ref[0] = jnp.concatenate(outs, axis=0).T.astype(o_ref.dtype)


def _nsa_attn(qt, kcmp, vcmpt, kslc, vslct, kwin, vwint, gatet, ovl, b, s):
    tq = TQ_ATTN
    assert tq == KEY_CHUNK and s % tq == 0 and WINDOW % KEY_CHUNK == 0
    n_slc = s // SLC_BLOCK
    assert n_slc <= NSA_HEAD_DIM
    g, dh = NSA_KV_GROUPS, NSA_HEAD_DIM
    nc = kcmp.shape[2]
    whole = lambda a: pl.BlockSpec((1,) + a.shape[1:], lambda bi, i: (bi,) + (0,) * (a.ndim - 1))
    return pl.pallas_call(
        functools.partial(_nsa_attn_kernel, tq=tq, n_slc=n_slc),
        out_shape=jax.ShapeDtypeStruct((b, s, NSA_HEADS * dh), BF16),
        grid=(b, s // tq),
        in_specs=[pl.BlockSpec((1, NSA_HEADS, dh, tq), lambda bi, i: (bi, 0, 0, i)),
                  whole(kcmp), whole(vcmpt), whole(kslc), whole(vslct), whole(kwin), whole(vwint),
                  pl.BlockSpec((1, g, GATE_ROWS, tq), lambda bi, i: (bi, 0, 0, i)),
                  pl.BlockSpec(ovl.shape, lambda bi, i: (0, 0))],
        out_specs=pl.BlockSpec((1, tq, NSA_HEADS * dh), lambda bi, i: (bi, i, 0)),
        scratch_shapes=[pltpu.VMEM((NSA_HEADS, 2 * dh, tq), BF16)] + _flash_scratch(2 * NSA_HEADS, tq),
        compiler_params=_params("parallel", "arbitrary"),
        name="nsa_attn",
    )(qt, kcmp, vcmpt, kslc, vslct, kwin, vwint, gatet, ovl)


def _merge_kernel(x_ref, oa_ref, cv_ref, on_ref, gm_ref, woa_ref, wcv_ref, bcv_ref, won_ref, wout_ref, o_ref):
    d = x_ref.shape[1]
    o_a = _dot(oa_ref[...], woa_ref[...])
    o_b = _dot(cv_ref[...], wcv_ref[...]) + bcv_ref[...]
    o_c = _dot(on_ref[...], won_ref[...])
    gate = lambda k: jax.nn.sigmoid(gm_ref[:, k * d:(k + 1) * d].astype(F32))
    mix = gate(0) * o_a + gate(1) * o_b + gate(2) * o_c
    o_ref[...] = x_ref[...] + _dot(mix.astype(BF16), wout_ref[...])


def _merge(x2d, oa, cv, on, gm, woa, wcv, bcv, won, wout):
    t, d = x2d.shape
    tm = min(TM_MERGE, t)
    row = lambda a: pl.BlockSpec((tm, a.shape[1]), lambda i: (i, 0))
    full = lambda a: pl.BlockSpec(a.shape, lambda i: (0,) * a.ndim)
    return pl.pallas_call(
        _merge_kernel,
        out_shape=jax.ShapeDtypeStruct((t, d), F32),
        grid=(t // tm,),
        in_specs=[row(x2d), row(oa), row(cv), row(on), row(gm), full(woa), full(wcv), full(bcv), full(won), full(wout)],
        out_specs=pl.BlockSpec((tm, d), lambda i: (i, 0)),
        compiler_params=_params("parallel"),
        name="merge",
    )(x2d, oa, cv, on, gm, woa, wcv, bcv, won, wout)


def _ffn_kernel(x_ref, g_ref, w1_ref, w2_ref, o_ref, h_ref, acc_ref):
    j = pl.program_id(1)

    @pl.when(j == 0)
    def _():
        x = x_ref[...]
        h_ref[...] = _rms(x, g_ref[...]).astype(h_ref.dtype)
        acc_ref[...] = x

    a = jnp.maximum(_dot(h_ref[...], w1_ref[...]), 0.0)
    acc_ref[...] += _dot((a * a).astype(BF16), w2_ref[...])

    @pl.when(j == pl.num_programs(1) - 1)
    def _():
        o_ref[...] = acc_ref[...]


def _ffn(x2d, g, w1, w2):
    t, d = x2d.shape
    f = w1.shape[1]
    tm = min(TM_FFN, t)
    tf = min(TF_FFN, f)
    return pl.pallas_call(
        _ffn_kernel,
        out_shape=jax.ShapeDtypeStruct((t, d), F32),
        grid=(t // tm, f // tf),
        in_specs=[pl.BlockSpec((tm, d), lambda i, j: (i, 0)),
                  pl.BlockSpec((1, d), lambda i, j: (0, 0)),
                  pl.BlockSpec((d, tf), lambda i, j: (0, j)),
                  pl.BlockSpec((tf, d), lambda i, j: (j, 0))],
        out_specs=pl.BlockSpec((tm, d), lambda i, j: (i, 0)),
        scratch_shapes=[pltpu.VMEM((tm, d), BF16), pltpu.VMEM((tm, d), F32)],
        compiler_params=_params("parallel", "arbitrary"),
        name="ffn",
    )(x2d, g, w1, w2)


def _rope_tables(pos, rot_dim, group, width=LANES):
    half = rot_dim // 2
    inv = jnp.power(jnp.float32(ROPE_THETA), -jnp.arange(half, dtype=jnp.float32) * (2.0 / rot_dim))
    ang = pos.astype(jnp.float32)[:, None] * inv
    n = ang.shape[0]
    cos = jnp.concatenate([jnp.cos(ang), jnp.cos(ang), jnp.ones((n, group - rot_dim), F32)], axis=1)
    sin = jnp.concatenate([-jnp.sin(ang), jnp.sin(ang), jnp.zeros((n, group - rot_dim), F32)], axis=1)
    reps = width // group
    return jnp.tile(cos, (1, reps)), jnp.tile(sin, (1, reps))


def _pad_cols(w, width):
    return jnp.pad(w, ((0, 0), (0, width - w.shape[1])))


def _pad_last(w, width):
    return jnp.pad(w, [(0, 0)] * (w.ndim - 1) + [(0, width - w.shape[-1])])


def _rope_partner(w):
    half = w.shape[-1] // 2
    return jnp.concatenate([w[..., half:], w[..., :half]], axis=-1)


def _overlap_t(s):
    n_cmp_pad = s // CMP_STRIDE
    n_slc = s // SLC_BLOCK
    c_start = np.arange(n_cmp_pad) * CMP_STRIDE
    s_start = np.arange(n_slc) * SLC_BLOCK
    ov = ((c_start[None, :] < s_start[:, None] + SLC_BLOCK) & (c_start[None, :] + CMP_BLOCK > s_start[:, None]))
    out = np.zeros((NSA_HEAD_DIM, n_cmp_pad), np.float32)
    out[:n_slc] = ov
    return jnp.asarray(out, BF16)


def _compress_weights(w1, w2):
    g, dh, hid = NSA_KV_GROUPS, NSA_HEAD_DIM, CMP_HIDDEN
    w1r = w1.reshape(2, CMP_STRIDE, dh, hid)
    wide = jnp.zeros((2, CMP_STRIDE, g, dh, g, hid), w1.dtype)
    for gi in range(g):
        wide = wide.at[:, :, gi, :, gi, :].set(w1r)
    wide = wide.reshape(2, CMP_STRIDE * g * dh, g * hid).astype(BF16)
    w2w = jnp.zeros((g, hid, g, dh), w2.dtype)
    for gi in range(g):
        w2w = w2w.at[gi, :, gi, :].set(w2)
    return wide[0], wide[1], w2w.reshape(g * hid, g * dh).astype(BF16)


def _compress_pe(pe):
    pe2 = pe.reshape(2, CMP_STRIDE, 1, NSA_HEAD_DIM)
    return jnp.broadcast_to(pe2, (2, CMP_STRIDE, NSA_KV_GROUPS, NSA_HEAD_DIM)).reshape(2, -1).astype(F32)


def kernel(x, g_mix, w_in, g_cq, g_ckv, w_uq, w_ukv, g_q_mla, g_k_mla, w_o_mla, b_glu, w_dw, b_dw, g_conv_ln, b_conv_ln, w_conv_out, b_conv_out, pe_cmp_k, pe_cmp_v, w_cmp_k1, w_cmp_k2, w_cmp_v1, w_cmp_v2, g_q_nsa, g_k_nsa, w_o_nsa, w_out, g_ffn, w_ff1, w_ff2):
    b, s, d = x.shape
    depth = w_in.shape[0]
    t = b * s
    q_rank, kv_rank = g_cq.shape[1], g_ckv.shape[1]
    nh, hd = NSA_HEADS, NSA_HEAD_DIM
    ngd = NSA_KV_GROUPS * hd
    assert ngd == LANES and s % CMP_STRIDE == 0

    pos = jnp.arange(s)
    cos_mla, sin_mla = _rope_tables(pos, MLA_ROPE_DIM, LANES)
    cos_nsa, sin_nsa = _rope_tables(pos, NSA_ROPE_DIM, hd)
    cmp_pos = jnp.arange(s // CMP_STRIDE) * CMP_STRIDE + CMP_BLOCK - 1
    cos_cmp, sin_cmp = _rope_tables(cmp_pos, NSA_ROPE_DIM, hd)
    ovl = _overlap_t(s)

    splits = (q_rank, kv_rank, MLA_ROPE_DIM, 2 * CONV_CH, nh * hd, 6 * ngd, 3 * nh, 3 * d)
    offs = np.concatenate([[0], np.cumsum(splits)])
    widths = (q_rank, kv_rank, LANES, LANES, 2 * CONV_CH, nh * hd) + (LANES,) * 6 + (NSA_KV_GROUPS * LANES, 3 * d)
    dtypes = (F32,) * (len(widths) - 1) + (BF16,)

    x2d = x.reshape(t, d)
    for l in range(depth):
        wl = w_in[l]
        seg = [wl[:, offs[k]:offs[k + 1]] for k in range(len(splits))]
        gate_cols = [_pad_cols(seg[6][:, gi * NSA_HPG * 3:(gi + 1) * NSA_HPG * 3], LANES) for gi in range(NSA_KV_GROUPS)]
        w_in_p = jnp.concatenate(
            [seg[0], seg[1], _pad_cols(seg[2], LANES), _pad_cols(_rope_partner(seg[2]), LANES), seg[3], seg[4], seg[5]]
            + gate_cols + [seg[7]], axis=1).astype(BF16)
        (cq, ckv, kr, krot, u2, qn, kc, vc, ks, vs, kw, vw, gn, gm) = _inproj(
            x2d, g_mix[l][None, :], w_in_p, widths, dtypes)

        wuq = w_uq[l].reshape(q_rank, MLA_HEADS, MLA_QK_DIM)
        wuq = jnp.concatenate([_pad_last(wuq, LANES), _pad_last(_rope_partner(wuq[..., :MLA_ROPE_DIM]), LANES)], axis=1)
        wuq = wuq.reshape(q_rank, 2 * MLA_HEADS * LANES).astype(BF16)
        wukv = w_ukv[l].reshape(kv_rank, MLA_HEADS, MLA_NOPE_DIM + V_DIM)
        wuk = jnp.pad(wukv[:, :, :MLA_NOPE_DIM], ((0, 0), (0, 0), (MLA_ROPE_DIM, LANES - MLA_QK_DIM)))
        wuk = wuk.reshape(kv_rank, MLA_HEADS * LANES).astype(BF16)
        wuv = wukv[:, :, MLA_NOPE_DIM:].reshape(kv_rank, MLA_HEADS * V_DIM).astype(BF16)
        gq = g_q_mla[l] * (MLA_QK_DIM ** -0.5 * LOG2_E)
        gq = jnp.stack([_pad_last(gq, LANES), _pad_last(_rope_partner(gq[:MLA_ROPE_DIM]), LANES)])
        gk = jnp.stack([_pad_last(g_k_mla[l], LANES), _pad_last(_rope_partner(g_k_mla[l][:MLA_ROPE_DIM]), LANES)])
        qt_m, k_m, vt_m = _mla_prep(cq, ckv, kr, krot, cos_mla, sin_mla, g_cq[l][None, :], g_ckv[l][None, :],
                                    wuq, wuk, wuv, gq, gk, b, s)
        o_mla = _mla_attn(qt_m, k_m, vt_m).reshape(t, MLA_HEADS * V_DIM)

        wdw = jnp.pad(w_dw[l], ((0, CONV_HALO - CONV_WIDTH), (0, 0)))
        o_cv = _conv(u2, b_glu[l][None, :], wdw, b_dw[l][None, :], g_conv_ln[l][None, :], b_conv_ln[l][None, :],
                     b, s).reshape(t, CONV_CH)

        gqn = jnp.tile(g_q_nsa[l] * (hd ** -0.5 * LOG2_E), 2)[None, :]
        gkn = jnp.tile(g_k_nsa[l], 2)[None, :]
        qt_n, k_slc, vt_slc, k_win, vt_win, gate_t = _nsa_prep(qn, ks, vs, kw, vw, gn, cos_nsa, sin_nsa, gqn, gkn, b, s)
        wka, wkb, wk2 = _compress_weights(w_cmp_k1[l], w_cmp_k2[l])
        wva, wvb, wv2 = _compress_weights(w_cmp_v1[l], w_cmp_v2[l])
        k_cmp, vt_cmp = _compress(kc, vc, _compress_pe(pe_cmp_k[l]), _compress_pe(pe_cmp_v[l]),
                                  wka, wkb, wk2, wva, wvb, wv2, gkn, cos_cmp, sin_cmp, b, s)
        o_nsa = _nsa_attn(qt_n, k_cmp, vt_cmp, k_slc, vt_slc, k_win, vt_win, gate_t, ovl, b, s).reshape(t, nh * hd)

        x2d = _merge(x2d, o_mla, o_cv, o_nsa, gm, w_o_mla[l].astype(BF16), w_conv_out[l].astype(BF16),
                     b_conv_out[l][None, :], w_o_nsa[l].astype(BF16), w_out[l].astype(BF16))
        x2d = _ffn(x2d, g_ffn[l][None, :], w1=w_ff1[l].astype(BF16), w2=w_ff2[l].astype(BF16))
    return x2d.reshape(b, s, d)
```

```python
import functools
import math

import numpy as np
import jax
import jax.numpy as jnp
from jax import lax
from jax.experimental import pallas as pl
from jax.experimental.pallas import tpu as pltpu

F32 = jnp.float32
BF16 = jnp.bfloat16

ROPE_THETA = 500000.0
NORM_EPS = 1e-6
MLA_HEADS = 8
MLA_NOPE_DIM = 64
MLA_ROPE_DIM = 32
MLA_QK_DIM = MLA_NOPE_DIM + MLA_ROPE_DIM
CONV_CH = 512
CONV_WIDTH = 31
NSA_HEADS = 8
NSA_KV_GROUPS = 2
NSA_HEAD_DIM = 64
NSA_ROPE_DIM = NSA_HEAD_DIM // 4
NSA_HPG = NSA_HEADS // NSA_KV_GROUPS
CMP_BLOCK = 32
CMP_STRIDE = 16
CMP_HIDDEN = 128
SLC_BLOCK = 64
SLC_TOP_N = 16
WINDOW = 512
FORCED_BLOCK_SCORE = 1e6
V_DIM = 64

LANES = 128
SUBLANES = 8
BF16_SUBLANES = 16
VMEM_LIMIT_BYTES = 56 * 1024 * 1024

MASK_VALUE = -1e30
LOG2_E = math.log2(math.e)
V_AUG = V_DIM + BF16_SUBLANES

TM_PROJ = 256
TM_PREP = 512
TQ_ATTN = 256
KEY_CHUNK = 256
SCORE_LOOKAHEAD = 6
TS_CONV = 256
CONV_HALO = 32
CONV_CHUNK = 64
TM_MERGE = 512
TM_FFN = 1024
TF_FFN = 512


def _dot(a, b):
    return jnp.dot(a, b, preferred_element_type=F32)


def _rms(x, g):
    return x * lax.rsqrt(jnp.mean(x * x, axis=-1, keepdims=True) + NORM_EPS) * g


def _params(*sem):
    return pltpu.CompilerParams(dimension_semantics=sem, vmem_limit_bytes=VMEM_LIMIT_BYTES)


def _flash_scratch(n_chains, n):
    return [pltpu.VMEM((n_chains, 1, n), F32), pltpu.VMEM((n_chains, V_AUG, n), F32)]


def _flash_reset(st):
    m_ref, acc_ref = st
    m_ref[...] = jnp.full(m_ref.shape, MASK_VALUE, F32)
    acc_ref[...] = jnp.zeros(acc_ref.shape, F32)


def _flash_steps(st, ops, lookahead=None):
    m_ref, acc_ref = st

    def softmax(j, s):
        n, _, _, _, mask, bias = ops[j]
        if mask is not None:
            s = jnp.where(mask, s, MASK_VALUE)
        if bias is not None:
            s = s + bias
        m = m_ref[n]
        m_new = jnp.maximum(m, jnp.max(s, axis=0, keepdims=True))
        m_ref[n] = m_new
        return jnp.exp2(m - m_new), jnp.exp2(s - m_new).astype(BF16)

    if lookahead is None:
        scores = [_dot(op[2], op[1]) for op in ops]
        probs = [softmax(j, s) for j, s in enumerate(scores)]
        pending = [(op[0], alpha, _dot(op[3], p)) for op, (alpha, p) in zip(ops, probs)]
    else:
        scores = {}
        pending = []
        for t in range(len(ops) + lookahead):
            if t < len(ops):
                scores[t] = _dot(ops[t][2], ops[t][1])
            j = t - lookahead
            if j >= 0:
                alpha, p = softmax(j, scores.pop(j))
                pending.append((ops[j][0], alpha, _dot(ops[j][3], p)))
    for n, alpha, pv in pending:
        acc_ref[n] = alpha * acc_ref[n] + pv


def _flash_result(st, n):
    _, acc_ref = st
    return acc_ref[n, 0:V_DIM, :] / acc_ref[n, V_DIM:V_DIM + 1, :]


def _causal_sweep(st, chains, i, tq, extra_ops_fn=None):
    _flash_reset(st)

    def ops_for(c, mask):
        return [(n, qt_fn(), k_fn(c), vt_fn(c), mask, None) for n, (qt_fn, k_fn, vt_fn) in enumerate(chains)]

    def pair(j, carry):
        _flash_steps(st, ops_for(2 * j, None) + ops_for(2 * j + 1, None), SCORE_LOOKAHEAD)
        return carry

    lax.fori_loop(0, i // 2, pair, 0)
    key = lax.broadcasted_iota(jnp.int32, (KEY_CHUNK, tq), 0)
    qry = lax.broadcasted_iota(jnp.int32, (KEY_CHUNK, tq), 1)
    extra = extra_ops_fn if extra_ops_fn is not None else (lambda: [])

    @pl.when(i % 2 == 1)
    def _():
        _flash_steps(st, ops_for(i - 1, None) + ops_for(i, key <= qry) + extra())

    @pl.when(i % 2 == 0)
    def _():
        _flash_steps(st, ops_for(i, key <= qry) + extra())


def _ones_rows(n):
    return jnp.where(lax.broadcasted_iota(jnp.int32, (V_AUG - V_DIM, n), 0) == 0, 1.0, 0.0).astype(BF16)


def _store_vt(v, vt_ref, n_heads):
    ones = _ones_rows(KEY_CHUNK)
    for p in range(n_heads // 2):
        t = v[:, p * LANES:(p + 1) * LANES].T.astype(vt_ref.dtype)
        for c in range(vt_ref.shape[2]):
            for e in range(2):
                vt_ref[0, 2 * p + e, c, 0:V_DIM, :] = t[e * V_DIM:(e + 1) * V_DIM, c * KEY_CHUNK:(c + 1) * KEY_CHUNK]
                vt_ref[0, 2 * p + e, c, V_DIM:V_AUG, :] = ones


def _vt_shape(b, n, s):
    return jax.ShapeDtypeStruct((b, n, s // KEY_CHUNK, V_AUG, KEY_CHUNK), BF16)


def _inproj_kernel(x_ref, g_ref, w_ref, *out_refs, widths):
    h = _rms(x_ref[...], g_ref[...]).astype(BF16)
    off = 0
    for o_ref, wd in zip(out_refs, widths):
        o_ref[...] = _dot(h, w_ref[:, off:off + wd]).astype(o_ref.dtype)
        off += wd


def _inproj(x2d, g, w, widths, dtypes):
    t, d = x2d.shape
    n = w.shape[1]
    return pl.pallas_call(
        functools.partial(_inproj_kernel, widths=widths),
        out_shape=[jax.ShapeDtypeStruct((t, wd), dt) for wd, dt in zip(widths, dtypes)],
        grid=(t // TM_PROJ,),
        in_specs=[pl.BlockSpec((TM_PROJ, d), lambda i: (i, 0)),
                  pl.BlockSpec((1, d), lambda i: (0, 0)),
                  pl.BlockSpec((d, n), lambda i: (0, 0), pipeline_mode=pl.Buffered(1))],
        out_specs=[pl.BlockSpec((TM_PROJ, wd), lambda i: (i, 0)) for wd in widths],
        compiler_params=_params("parallel"),
        name="inproj",
    )(x2d, g, w)


def _mla_prep_kernel(cq_ref, ckv_ref, kr_ref, krot_ref, cos_ref, sin_ref, gcq_ref, gckv_ref,
                     wuq_ref, wuk_ref, wuv_ref, gq_ref, gk_ref, qt_ref, k_ref, vt_ref):
    cos = cos_ref[...]
    sin = sin_ref[...]
    q_cos, q_sin = gq_ref[0:1, :] * cos, gq_ref[1:2, :] * sin
    k_cos, k_sin = gk_ref[0:1, :] * cos, gk_ref[1:2, :] * sin
    inv_dim = 1.0 / MLA_QK_DIM
    hw = MLA_HEADS * LANES
    qq = _dot(_rms(cq_ref[...], gcq_ref[...]).astype(BF16), wuq_ref[...])
    for h in range(MLA_HEADS):
        qh = qq[:, h * LANES:(h + 1) * LANES]
        qp = qq[:, hw + h * LANES:hw + (h + 1) * LANES]
        r = lax.rsqrt(jnp.sum(qh * qh, axis=-1, keepdims=True) * inv_dim + NORM_EPS)
        qt_ref[0, h] = ((qh * q_cos + qp * q_sin) * r).T.astype(qt_ref.dtype)

    ckvn = _rms(ckv_ref[...], gckv_ref[...]).astype(BF16)
    kn = _dot(ckvn, wuk_ref[...])
    v = _dot(ckvn, wuv_ref[...])
    kr = kr_ref[...]
    kr_rot = krot_ref[...] * k_sin
    for h in range(MLA_HEADS):
        kh = kn[:, h * LANES:(h + 1) * LANES] + kr
        r = lax.rsqrt(jnp.sum(kh * kh, axis=-1, keepdims=True) * inv_dim + NORM_EPS)
        k_ref[0, h] = ((kh * k_cos + kr_rot) * r).astype(k_ref.dtype)
    _store_vt(v, vt_ref, MLA_HEADS)


def _mla_prep(cq, ckv, kr, krot, cos, sin, gcq, gckv, wuq, wuk, wuv, gq, gk, b, s):
    tm = min(TM_PREP, s)
    nb = s // tm
    row = lambda width: pl.BlockSpec((tm, width), lambda bi, i: (bi * nb + i, 0))
    tab = pl.BlockSpec((tm, LANES), lambda bi, i: (i, 0))
    full = lambda a: pl.BlockSpec(a.shape, lambda bi, i: (0,) * a.ndim)
    return pl.pallas_call(
        _mla_prep_kernel,
        out_shape=[jax.ShapeDtypeStruct((b, MLA_HEADS, LANES, s), BF16),
                   jax.ShapeDtypeStruct((b, MLA_HEADS, s, LANES), BF16),
                   _vt_shape(b, MLA_HEADS, s)],
        grid=(b, nb),
        in_specs=[row(cq.shape[1]), row(ckv.shape[1]), row(LANES), row(LANES), tab, tab,
                  full(gcq), full(gckv), full(wuq), full(wuk), full(wuv), full(gq), full(gk)],
        out_specs=[pl.BlockSpec((1, MLA_HEADS, LANES, tm), lambda bi, i: (bi, 0, 0, i)),
                   pl.BlockSpec((1, MLA_HEADS, tm, LANES), lambda bi, i: (bi, 0, i, 0)),
                   pl.BlockSpec((1, MLA_HEADS, tm // KEY_CHUNK, V_AUG, KEY_CHUNK), lambda bi, i: (bi, 0, i, 0, 0))],
        compiler_params=_params("parallel", "parallel"),
        name="mla_prep",
    )(cq, ckv, kr, krot, cos, sin, gcq, gckv, wuq, wuk, wuv, gq, gk)


def _mla_attn_kernel(qt_ref, k_ref, vt_ref, o_ref, m_ref, acc_ref, *, tq):
    i = pl.program_id(1)
    st = (m_ref, acc_ref)
    chains = [(lambda h=h: qt_ref[0, h],
               lambda c, h=h: k_ref[0, h, pl.ds(pl.multiple_of(c * KEY_CHUNK, KEY_CHUNK), KEY_CHUNK), :],
               lambda c, h=h: vt_ref[0, h, c]) for h in range(MLA_HEADS)]
    _causal_sweep(st, chains, i, tq)
    outs = [_flash_result(st, h) for h in range(MLA_HEADS)]
    o_ref[0] = jnp.concatenate(outs, axis=0).T.astype(o_ref.dtype)


def _mla_attn(qt, k, vt):
    b, nh, s, _ = k.shape
    tq = TQ_ATTN
    assert tq == KEY_CHUNK and s % tq == 0
    return pl.pallas_call(
        functools.partial(_mla_attn_kernel, tq=tq),
        out_shape=jax.ShapeDtypeStruct((b, s, nh * V_DIM), BF16),
        grid=(b, s // tq),
        in_specs=[pl.BlockSpec((1, nh, LANES, tq), lambda bi, i: (bi, 0, 0, i)),
                  pl.BlockSpec((1, nh, s, LANES), lambda bi, i: (bi, 0, 0, 0)),
                  pl.BlockSpec((1, nh, s // KEY_CHUNK, V_AUG, KEY_CHUNK), lambda bi, i: (bi, 0, 0, 0, 0))],
        out_specs=pl.BlockSpec((1, tq, nh * V_DIM), lambda bi, i: (bi, i, 0)),
        scratch_shapes=_flash_scratch(nh, tq),
        compiler_params=_params("parallel", "arbitrary"),
        name="mla_attn",
    )(qt, k, vt)


def _conv_kernel(u_ref, halo_ref, bglu_ref, wdw_ref, bdw_ref, gln_ref, bln_ref, o_ref, ext_ref, *, ts):
    i = pl.program_id(1)
    bglu = bglu_ref[...]

    def glu(u2):
        u2 = u2 + bglu
        return u2[:, :CONV_CH] * jax.nn.sigmoid(u2[:, CONV_CH:])

    ext_ref[0, 0:CONV_HALO, :] = jnp.where(i > 0, glu(halo_ref[0]), 0.0)
    ext_ref[0, CONV_HALO:, :] = glu(u_ref[0])
    rows = ts + CONV_HALO - SUBLANES
    for b in range(1, SUBLANES):
        ext_ref[b, 0:rows, :] = ext_ref[0, b:b + rows, :]
    wdw = wdw_ref[...]
    shift = CONV_HALO - (CONV_WIDTH - 1)
    for r0 in range(0, ts, CONV_CHUNK):
        acc = jnp.zeros((CONV_CHUNK, CONV_CH), F32)
        for k in range(CONV_WIDTH):
            a, b = divmod(shift + k, SUBLANES)
            acc = acc + wdw[k:k + 1, :] * ext_ref[b, r0 + SUBLANES * a:r0 + SUBLANES * a + CONV_CHUNK, :]
        acc = acc + bdw_ref[...]
        mu = jnp.mean(acc, axis=-1, keepdims=True)
        d = acc - mu
        var = jnp.mean(d * d, axis=-1, keepdims=True)
        y = d * lax.rsqrt(var + NORM_EPS) * gln_ref[...] + bln_ref[...]
        o_ref[0, r0:r0 + CONV_CHUNK, :] = (y * jax.nn.sigmoid(y)).astype(o_ref.dtype)


def _conv(u2, bglu, wdw, bdw, gln, bln, b, s):
    ts = min(TS_CONV, s)
    u3 = u2.reshape(b, s, 2 * CONV_CH)
    hb = ts // CONV_HALO
    full = lambda a: pl.BlockSpec(a.shape, lambda bi, i: (0,) * a.ndim)
    return pl.pallas_call(
        functools.partial(_conv_kernel, ts=ts),
        out_shape=jax.ShapeDtypeStruct((b, s, CONV_CH), BF16),
        grid=(b, s // ts),
        in_specs=[pl.BlockSpec((1, ts, 2 * CONV_CH), lambda bi, i: (bi, i, 0)),
                  pl.BlockSpec((1, CONV_HALO, 2 * CONV_CH), lambda bi, i: (bi, jnp.maximum(i * hb - 1, 0), 0)),
                  full(bglu), full(wdw), full(bdw), full(gln), full(bln)],
        out_specs=pl.BlockSpec((1, ts, CONV_CH), lambda bi, i: (bi, i, 0)),
        scratch_shapes=[pltpu.VMEM((SUBLANES, ts + CONV_HALO, CONV_CH), F32)],
        compiler_params=_params("parallel", "parallel"),
        name="conv_mixer",
    )(u3, u3, bglu, wdw, bdw, gln, bln)


def _pair_norm_rope(x, g, cos, sin):
    lane = lax.broadcasted_iota(jnp.int32, x.shape, 1)
    lo = lane < NSA_HEAD_DIM
    x2 = x * x
    s_lo = jnp.sum(jnp.where(lo, x2, 0.0), axis=-1, keepdims=True)
    s_hi = jnp.sum(jnp.where(lo, 0.0, x2), axis=-1, keepdims=True)
    ss = jnp.where(lo, s_lo, s_hi) * (1.0 / NSA_HEAD_DIM)
    y = x * lax.rsqrt(ss + NORM_EPS) * g
    half = NSA_ROPE_DIM // 2
    first = (lane & (NSA_HEAD_DIM - 1)) < half
    rot = jnp.where(first, pltpu.roll(y, LANES - half, 1), pltpu.roll(y, half, 1))
    return y * cos + rot * sin


def _nsa_prep_kernel(qn_ref, ks_ref, vs_ref, kw_ref, vw_ref, gate_ref, cos_ref, sin_ref, gq_ref, gk_ref,
                     qt_ref, kslc_ref, vslct_ref, kwin_ref, vwint_ref, gatet_ref, *, tm):
    i = pl.program_id(1)
    cos = cos_ref[...]
    sin = sin_ref[...]
    dh = NSA_HEAD_DIM
    for p in range(NSA_HEADS // 2):
        yt = _pair_norm_rope(qn_ref[:, p * LANES:(p + 1) * LANES], gq_ref[...], cos, sin).T.astype(qt_ref.dtype)
        qt_ref[0, 2 * p] = yt[0:dh]
        qt_ref[0, 2 * p + 1] = yt[dh:2 * dh]
    ks = _pair_norm_rope(ks_ref[...], gk_ref[...], cos, sin).astype(kslc_ref.dtype)
    kw = _pair_norm_rope(kw_ref[...], gk_ref[...], cos, sin).astype(kwin_ref.dtype)
    blk = (i * tm + lax.broadcasted_iota(jnp.int32, (tm, dh), 0)) // SLC_BLOCK
    onehot = jnp.where(lax.broadcasted_iota(jnp.int32, (tm, dh), 1) == blk, 1.0, 0.0).astype(kslc_ref.dtype)
    for g in range(NSA_KV_GROUPS):
        kslc_ref[0, g, :, 0:dh] = ks[:, g * dh:(g + 1) * dh]
        kslc_ref[0, g, :, dh:2 * dh] = onehot
        kwin_ref[0, g] = kw[:, g * dh:(g + 1) * dh]
    _store_vt(vs_ref[...], vslct_ref, NSA_KV_GROUPS)
    _store_vt(vw_ref[...], vwint_ref, NSA_KV_GROUPS)
    for g in range(NSA_KV_GROUPS):
        gatet_ref[0, g] = gate_ref[:, g * LANES:(g + 1) * LANES].T[0:GATE_ROWS]


GATE_ROWS = 16


def _nsa_prep(qn, ks, vs, kw, vw, gates, cos, sin, gq, gk, b, s):
    tm = min(TM_PREP, s)
    nb = s // tm
    row = lambda width: pl.BlockSpec((tm, width), lambda bi, i: (bi * nb + i, 0))
    tab = pl.BlockSpec((tm, LANES), lambda bi, i: (i, 0))
    full = lambda a: pl.BlockSpec(a.shape, lambda bi, i: (0,) * a.ndim)
    g, dh = NSA_KV_GROUPS, NSA_HEAD_DIM
    vt_spec = pl.BlockSpec((1, g, tm // KEY_CHUNK, V_AUG, KEY_CHUNK), lambda bi, i: (bi, 0, i, 0, 0))
    return pl.pallas_call(
        functools.partial(_nsa_prep_kernel, tm=tm),
        out_shape=[jax.ShapeDtypeStruct((b, NSA_HEADS, dh, s), BF16),
                   jax.ShapeDtypeStruct((b, g, s, 2 * dh), BF16),
                   _vt_shape(b, g, s),
                   jax.ShapeDtypeStruct((b, g, s, dh), BF16),
                   _vt_shape(b, g, s),
                   jax.ShapeDtypeStruct((b, g, GATE_ROWS, s), F32)],
        grid=(b, nb),
        in_specs=[row(qn.shape[1]), row(LANES), row(LANES), row(LANES), row(LANES), row(g * LANES),
                  tab, tab, full(gq), full(gk)],
        out_specs=[pl.BlockSpec((1, NSA_HEADS, dh, tm), lambda bi, i: (bi, 0, 0, i)),
                   pl.BlockSpec((1, g, tm, 2 * dh), lambda bi, i: (bi, 0, i, 0)),
                   vt_spec,
                   pl.BlockSpec((1, g, tm, dh), lambda bi, i: (bi, 0, i, 0)),
                   vt_spec,
                   pl.BlockSpec((1, g, GATE_ROWS, tm), lambda bi, i: (bi, 0, 0, i))],
        compiler_params=_params("parallel", "parallel"),
        name="nsa_prep",
    )(qn, ks, vs, kw, vw, gates, cos, sin, gq, gk)


def _compress_kernel(kc_ref, vc_ref, pek_ref, pev_ref, wka_ref, wkb_ref, wk2_ref, wva_ref, wvb_ref, wv2_ref,
                     gk_ref, cos_ref, sin_ref, kcmp_ref, vcmpt_ref):
    def comp(r_ref, pe_ref, wa_ref, wb_ref, w2_ref):
        r = r_ref[0]
        nr = r.shape[0]
        a = _dot((r + pe_ref[0:1, :]).astype(BF16), wa_ref[...])
        bb = _dot((r + pe_ref[1:2, :]).astype(BF16), wb_ref[...])
        hd = a + pltpu.roll(bb, nr - 1, 0)
        hd = hd * jax.nn.sigmoid(hd)
        return _dot(hd.astype(BF16), w2_ref[...])

    k = _pair_norm_rope(comp(kc_ref, pek_ref, wka_ref, wkb_ref, wk2_ref), gk_ref[...], cos_ref[...], sin_ref[...])
    vt = comp(vc_ref, pev_ref, wva_ref, wvb_ref, wv2_ref).T
    dh = NSA_HEAD_DIM
    for g in range(NSA_KV_GROUPS):
        kcmp_ref[0, g] = k[:, g * dh:(g + 1) * dh].astype(kcmp_ref.dtype)
        vcmpt_ref[0, g] = vt[g * dh:(g + 1) * dh].astype(vcmpt_ref.dtype)


def _compress(kc, vc, pek, pev, wka, wkb, wk2, wva, wvb, wv2, gk, cos, sin, b, s):
    nr = s // CMP_STRIDE
    width = CMP_STRIDE * LANES
    kc3 = kc.reshape(b, nr, width)
    vc3 = vc.reshape(b, nr, width)
    full = lambda a: pl.BlockSpec(a.shape, lambda bi: (0,) * a.ndim)
    blk = pl.BlockSpec((1, nr, width), lambda bi: (bi, 0, 0))
    g, dh = NSA_KV_GROUPS, NSA_HEAD_DIM
    return pl.pallas_call(
        _compress_kernel,
        out_shape=[jax.ShapeDtypeStruct((b, g, nr, dh), BF16), jax.ShapeDtypeStruct((b, g, dh, nr), BF16)],
        grid=(b,),
        in_specs=[blk, blk, full(pek), full(pev), full(wka), full(wkb), full(wk2), full(wva), full(wvb), full(wv2),
                  full(gk), full(cos), full(sin)],
        out_specs=[pl.BlockSpec((1, g, nr, dh), lambda bi: (bi, 0, 0, 0)),
                   pl.BlockSpec((1, g, dh, nr), lambda bi: (bi, 0, 0, 0))],
        compiler_params=_params("parallel"),
        name="nsa_compress",
    )(kc3, vc3, pek, pev, wka, wkb, wk2, wva, wvb, wv2, gk, cos, sin)


def _selection_bias(imp, i, tq, n_slc):
    nb = imp.shape[0]
    jrow = lax.broadcasted_iota(jnp.int32, (nb, tq), 0)
    blk_t = (i * tq + lax.broadcasted_iota(jnp.int32, (nb, tq), 1)) // SLC_BLOCK
    forced = (jrow == 0) | (jrow == blk_t) | (jrow == blk_t - 1)
    imp = jnp.where(forced, FORCED_BLOCK_SCORE, imp)
    imp = jnp.where(jrow <= blk_t, imp, -1.0)
    n_slab = -(-n_slc // SUBLANES)
    slabs = [imp[SUBLANES * c:SUBLANES * (c + 1), :] for c in range(n_slab)]
    ranks = [jnp.zeros((SUBLANES, tq), jnp.int32) for _ in range(n_slab)]
    sub = lax.broadcasted_iota(jnp.int32, (SUBLANES, tq), 0)
    for jp in range(n_slc):
        other = imp[jp:jp + 1, :]
        for c in range(n_slab):
            if SUBLANES * c > jp:
                ranks[c] = ranks[c] + jnp.where(other >= slabs[c], 1, 0)
            elif SUBLANES * (c + 1) - 1 <= jp:
                ranks[c] = ranks[c] + jnp.where(other > slabs[c], 1, 0)
            else:
                ge = jnp.where(other >= slabs[c], 1, 0)
                gt = jnp.where(other > slabs[c], 1, 0)
                ranks[c] = ranks[c] + jnp.where(sub > jp - SUBLANES * c, ge, gt)
    if nb > SUBLANES * n_slab:
        ranks.append(jnp.full((nb - SUBLANES * n_slab, tq), SLC_TOP_N, jnp.int32))
    rank = jnp.concatenate(ranks, axis=0)
    return jnp.where(rank < SLC_TOP_N, 0.0, MASK_VALUE)


def _nsa_attn_kernel(qt_ref, kc_ref, vct_ref, ks_ref, vst_ref, kw_ref, vwt_ref, gt_ref, ovl_ref, o_ref,
                     qa_ref, m_ref, acc_ref, *, tq, n_slc):
    i = pl.program_id(1)
    q0 = i * tq
    hpg, dh, ng = NSA_HPG, NSA_HEAD_DIM, NSA_KV_GROUPS
    st = (m_ref, acc_ref)
    heads = [(g, h) for g in range(ng) for h in range(hpg)]

    nc = kc_ref.shape[2]
    crow = lax.broadcasted_iota(jnp.int32, (nc, tq), 0)
    qcol = lax.broadcasted_iota(jnp.int32, (nc, tq), 1)
    cbias = jnp.where(crow * CMP_STRIDE + (CMP_BLOCK - 1) <= q0 + qcol, 0.0, MASK_VALUE)
    seen = jnp.where(q0 + lax.broadcasted_iota(jnp.int32, (1, tq), 1) >= CMP_BLOCK - 1, 1.0, 0.0)
    scores = [_dot(kc_ref[0, g], qt_ref[0, g * hpg + h]) for g, h in heads]
    probs = []
    for s in scores:
        s = s + cbias
        p = jnp.exp2(s - jnp.max(s, axis=0, keepdims=True))
        probs.append(p * (seen / jnp.maximum(jnp.sum(p, axis=0, keepdims=True), 1e-30)))
    o_cmp = [_dot(vct_ref[0, g], p.astype(BF16)) for (g, h), p in zip(heads, probs)]

    ovl = ovl_ref[...]
    for g in range(ng):
        ps = probs[g * hpg]
        for h in range(1, hpg):
            ps = ps + probs[g * hpg + h]
        hi = ps.astype(BF16)
        rem = ps - hi.astype(F32)
        mid = rem.astype(BF16)
        low = (rem - mid.astype(F32)).astype(BF16)
        imp = _dot(ovl, hi) + _dot(ovl, mid) + _dot(ovl, low)
        bias = _selection_bias(imp, i, tq, n_slc).astype(BF16)
        for h in range(hpg):
            qa_ref[g * hpg + h, 0:dh, :] = qt_ref[0, g * hpg + h]
            qa_ref[g * hpg + h, dh:2 * dh, :] = bias

    def kchunk(ref, g):
        return lambda c: ref[0, g, pl.ds(pl.multiple_of(c * KEY_CHUNK, KEY_CHUNK), KEY_CHUNK), :]

    nh = len(heads)
    n_back = WINDOW // KEY_CHUNK

    def win_ops():
        key = lax.broadcasted_iota(jnp.int32, (KEY_CHUNK, tq), 0)
        qry = lax.broadcasted_iota(jnp.int32, (KEY_CHUNK, tq), 1)
        ops = []
        for d in range(n_back, -1, -1):
            c = jnp.maximum(i - d, 0)
            mask = (key > qry) if d == n_back else ((key <= qry) if d == 0 else None)
            bias = jnp.where(i >= d, 0.0, MASK_VALUE) if d > 0 else None
            ops += [(nh + g * hpg + h, qt_ref[0, g * hpg + h], kchunk(kw_ref, g)(c), vwt_ref[0, g, c], mask, bias)
                    for g, h in heads]
        return ops

    _causal_sweep(st, [(lambda n=g * hpg + h: qa_ref[n], kchunk(ks_ref, g), lambda c, g=g: vst_ref[0, g, c])
                       for g, h in heads], i, tq, extra_ops_fn=win_ops)
    o_slc = [_flash_result(st, n) for n in range(nh)]
    o_win = [_flash_result(st, nh + n) for n in range(nh)]

    outs = []
    for g in range(ng):
        sg = jax.nn.sigmoid(gt_ref[0, g])
        for h in range(hpg):
            n = g * hpg + h
            outs.append(sg[3 * h:3 * h + 1] * o_cmp[n] + sg[3 * h + 1:3 * h + 2] * o_slc[n]
                        + sg[3 * h + 2:3 * h + 3] * o_win[n])
    o_ref[0] = jnp.concatenate(outs, axis=0).T.astype(o_ref.dtype)


def _nsa_attn(qt, kcmp, vcmpt, kslc, vslct, kwin, vwint, gatet, ovl, b, s):
    tq = TQ_ATTN
    assert tq == KEY_CHUNK and s % tq == 0 and WINDOW % KEY_CHUNK == 0
    n_slc = s // SLC_BLOCK
    assert n_slc <= NSA_HEAD_DIM
    g, dh = NSA_KV_GROUPS, NSA_HEAD_DIM
    nc = kcmp.shape[2]
    whole = lambda a: pl.BlockSpec((1,) + a.shape[1:], lambda bi, i: (bi,) + (0,) * (a.ndim - 1))
    return pl.pallas_call(
        functools.partial(_nsa_attn_kernel, tq=tq, n_slc=n_slc),
        out_shape=jax.ShapeDtypeStruct((b, s, NSA_HEADS * dh), BF16),
        grid=(b, s // tq),
        in_specs=[pl.BlockSpec((1, NSA_HEADS, dh, tq), lambda bi, i: (bi, 0, 0, i)),
                  whole(kcmp), whole(vcmpt), whole(kslc), whole(vslct), whole(kwin), whole(vwint),
                  pl.BlockSpec((1, g, GATE_ROWS, tq), lambda bi, i: (bi, 0, 0, i)),
                  pl.BlockSpec(ovl.shape, lambda bi, i: (0, 0))],
        out_specs=pl.BlockSpec((1, tq, NSA_HEADS * dh), lambda bi, i: (bi, i, 0)),
        scratch_shapes=[pltpu.VMEM((NSA_HEADS, 2 * dh, tq), BF16)] + _flash_scratch(2 * NSA_HEADS, tq),
        compiler_params=_params("parallel", "arbitrary"),
        name="nsa_attn",
    )(qt, kcmp, vcmpt, kslc, vslct, kwin, vwint, gatet, ovl)


def _merge_kernel(x_ref, oa_ref, cv_ref, on_ref, gm_ref, woa_ref, wcv_ref, bcv_ref, won_ref, wout_ref, o_ref):
    d = x_ref.shape[1]
    o_a = _dot(oa_ref[...], woa_ref[...])
    o_b = _dot(cv_ref[...], wcv_ref[...]) + bcv_ref[...]
    o_c = _dot(on_ref[...], won_ref[...])
    gate = lambda k: jax.nn.sigmoid(gm_ref[:, k * d:(k + 1) * d].astype(F32))
    mix = gate(0) * o_a + gate(1) * o_b + gate(2) * o_c
    o_ref[...] = x_ref[...] + _dot(mix.astype(BF16), wout_ref[...])


def _merge(x2d, oa, cv, on, gm, woa, wcv, bcv, won, wout):
    t, d = x2d.shape
    tm = min(TM_MERGE, t)
    row = lambda a: pl.BlockSpec((tm, a.shape[1]), lambda i: (i, 0))
    full = lambda a: pl.BlockSpec(a.shape, lambda i: (0,) * a.ndim)
    return pl.pallas_call(
        _merge_kernel,
        out_shape=jax.ShapeDtypeStruct((t, d), F32),
        grid=(t // tm,),
        in_specs=[row(x2d), row(oa), row(cv), row(on), row(gm), full(woa), full(wcv), full(bcv), full(won), full(wout)],
        out_specs=pl.BlockSpec((tm, d), lambda i: (i, 0)),
        compiler_params=_params("parallel"),
        name="merge",
    )(x2d, oa, cv, on, gm, woa, wcv, bcv, won, wout)


def _ffn_kernel(x_ref, g_ref, w1_ref, w2_ref, o_ref, h_ref, acc_ref):
    j = pl.program_id(1)

    @pl.when(j == 0)
    def _():
        x = x_ref[...]
        h_ref[...] = _rms(x, g_ref[...]).astype(h_ref.dtype)
        acc_ref[...] = x

    a = jnp.maximum(_dot(h_ref[...], w1_ref[...]), 0.0)
    acc_ref[...] += _dot((a * a).astype(BF16), w2_ref[...])

    @pl.when(j == pl.num_programs(1) - 1)
    def _():
        o_ref[...] = acc_ref[...]


def _ffn(x2d, g, w1, w2):
    t, d = x2d.shape
    f = w1.shape[1]
    tm = min(TM_FFN, t)
    tf = min(TF_FFN, f)
    return pl.pallas_call(
        _ffn_kernel,
        out_shape=jax.ShapeDtypeStruct((t, d), F32),
        grid=(t // tm, f // tf),
        in_specs=[pl.BlockSpec((tm, d), lambda i, j: (i, 0)),
                  pl.BlockSpec((1, d), lambda i, j: (0, 0)),
                  pl.BlockSpec((d, tf), lambda i, j: (0, j)),
                  pl.BlockSpec((tf, d), lambda i, j: (j, 0))],
        out_specs=pl.BlockSpec((tm, d), lambda i, j: (i, 0)),
        scratch_shapes=[pltpu.VMEM((tm, d), BF16), pltpu.VMEM((tm, d), F32)],
        compiler_params=_params("parallel", "arbitrary"),
        name="ffn",
    )(x2d, g, w1, w2)


def _rope_tables(pos, rot_dim, group, width=LANES):
    half = rot_dim // 2
    inv = jnp.power(jnp.float32(ROPE_THETA), -jnp.arange(half, dtype=jnp.float32) * (2.0 / rot_dim))
    ang = pos.astype(jnp.float32)[:, None] * inv
    n = ang.shape[0]
    cos = jnp.concatenate([jnp.cos(ang), jnp.cos(ang), jnp.ones((n, group - rot_dim), F32)], axis=1)
    sin = jnp.concatenate([-jnp.sin(ang), jnp.sin(ang), jnp.zeros((n, group - rot_dim), F32)], axis=1)
    reps = width // group
    return jnp.tile(cos, (1, reps)), jnp.tile(sin, (1, reps))


def _pad_cols(w, width):
    return jnp.pad(w, ((0, 0), (0, width - w.shape[1])))


def _pad_last(w, width):
    return jnp.pad(w, [(0, 0)] * (w.ndim - 1) + [(0, width - w.shape[-1])])


def _rope_partner(w):
    half = w.shape[-1] // 2
    return jnp.concatenate([w[..., half:], w[..., :half]], axis=-1)


def _overlap_t(s):
    n_cmp_pad = s // CMP_STRIDE
    n_slc = s // SLC_BLOCK
    c_start = np.arange(n_cmp_pad) * CMP_STRIDE
    s_start = np.arange(n_slc) * SLC_BLOCK
    ov = ((c_start[None, :] < s_start[:, None] + SLC_BLOCK) & (c_start[None, :] + CMP_BLOCK > s_start[:, None]))
    out = np.zeros((NSA_HEAD_DIM, n_cmp_pad), np.float32)
    out[:n_slc] = ov
    return jnp.asarray(out, BF16)


def _compress_weights(w1, w2):
    g, dh, hid = NSA_KV_GROUPS, NSA_HEAD_DIM, CMP_HIDDEN
    assert g == 2

    def block_diag(w):
        z = jnp.zeros_like(w)
        return jnp.concatenate([jnp.concatenate([w, z], axis=-1), jnp.concatenate([z, w], axis=-1)], axis=-2)

    w1r = w1.astype(BF16).reshape(2, CMP_STRIDE, dh, hid)
    wide = block_diag(w1r).reshape(2, CMP_STRIDE * g * dh, g * hid)
    return wide[0], wide[1], block_diag(w2.astype(BF16))


def _compress_pe(pe):
    pe2 = pe.reshape(2, CMP_STRIDE, 1, NSA_HEAD_DIM)
    return jnp.broadcast_to(pe2, (2, CMP_STRIDE, NSA_KV_GROUPS, NSA_HEAD_DIM)).reshape(2, -1).astype(F32)


def kernel(x, g_mix, w_in, g_cq, g_ckv, w_uq, w_ukv, g_q_mla, g_k_mla, w_o_mla, b_glu, w_dw, b_dw, g_conv_ln, b_conv_ln, w_conv_out, b_conv_out, pe_cmp_k, pe_cmp_v, w_cmp_k1, w_cmp_k2, w_cmp_v1, w_cmp_v2, g_q_nsa, g_k_nsa, w_o_nsa, w_out, g_ffn, w_ff1, w_ff2):
    b, s, d = x.shape
    depth = w_in.shape[0]
    t = b * s
    q_rank, kv_rank = g_cq.shape[1], g_ckv.shape[1]
    nh, hd = NSA_HEADS, NSA_HEAD_DIM
    ngd = NSA_KV_GROUPS * hd
    assert ngd == LANES and s % CMP_STRIDE == 0

    pos = jnp.arange(s)
    cos_mla, sin_mla = _rope_tables(pos, MLA_ROPE_DIM, LANES)
    cos_nsa, sin_nsa = _rope_tables(pos, NSA_ROPE_DIM, hd)
    cmp_pos = jnp.arange(s // CMP_STRIDE) * CMP_STRIDE + CMP_BLOCK - 1
    cos_cmp, sin_cmp = _rope_tables(cmp_pos, NSA_ROPE_DIM, hd)
    ovl = _overlap_t(s)

    splits = (q_rank, kv_rank, MLA_ROPE_DIM, 2 * CONV_CH, nh * hd, 6 * ngd, 3 * nh, 3 * d)
    offs = np.concatenate([[0], np.cumsum(splits)])
    widths = (q_rank, kv_rank, LANES, LANES, 2 * CONV_CH, nh * hd) + (LANES,) * 6 + (NSA_KV_GROUPS * LANES, 3 * d)
    dtypes = (F32,) * (len(widths) - 1) + (BF16,)

    x2d = x.reshape(t, d)
    for l in range(depth):
        wl = w_in[l]
        seg = [wl[:, offs[k]:offs[k + 1]] for k in range(len(splits))]
        gate_cols = [_pad_cols(seg[6][:, gi * NSA_HPG * 3:(gi + 1) * NSA_HPG * 3], LANES) for gi in range(NSA_KV_GROUPS)]
        w_in_p = jnp.concatenate(
            [seg[0], seg[1], _pad_cols(seg[2], LANES), _pad_cols(_rope_partner(seg[2]), LANES), seg[3], seg[4], seg[5]]
            + gate_cols + [seg[7]], axis=1).astype(BF16)
        (cq, ckv, kr, krot, u2, qn, kc, vc, ks, vs, kw, vw, gn, gm) = _inproj(
            x2d, g_mix[l][None, :], w_in_p, widths, dtypes)

        wuq = w_uq[l].reshape(q_rank, MLA_HEADS, MLA_QK_DIM)
        wuq = jnp.concatenate([_pad_last(wuq, LANES), _pad_last(_rope_partner(wuq[..., :MLA_ROPE_DIM]), LANES)], axis=1)
        wuq = wuq.reshape(q_rank, 2 * MLA_HEADS * LANES).astype(BF16)
        wukv = w_ukv[l].reshape(kv_rank, MLA_HEADS, MLA_NOPE_DIM + V_DIM)
        wuk = jnp.pad(wukv[:, :, :MLA_NOPE_DIM], ((0, 0), (0, 0), (MLA_ROPE_DIM, LANES - MLA_QK_DIM)))
        wuk = wuk.reshape(kv_rank, MLA_HEADS * LANES).astype(BF16)
        wuv = wukv[:, :, MLA_NOPE_DIM:].reshape(kv_rank, MLA_HEADS * V_DIM).astype(BF16)
        gq = g_q_mla[l] * (MLA_QK_DIM ** -0.5 * LOG2_E)
        gq = jnp.stack([_pad_last(gq, LANES), _pad_last(_rope_partner(gq[:MLA_ROPE_DIM]), LANES)])
        gk = jnp.stack([_pad_last(g_k_mla[l], LANES), _pad_last(_rope_partner(g_k_mla[l][:MLA_ROPE_DIM]), LANES)])
        qt_m, k_m, vt_m = _mla_prep(cq, ckv, kr, krot, cos_mla, sin_mla, g_cq[l][None, :], g_ckv[l][None, :],
                                    wuq, wuk, wuv, gq, gk, b, s)
        o_mla = _mla_attn(qt_m, k_m, vt_m).reshape(t, MLA_HEADS * V_DIM)

        wdw = jnp.pad(w_dw[l], ((0, CONV_HALO - CONV_WIDTH), (0, 0)))
        o_cv = _conv(u2, b_glu[l][None, :], wdw, b_dw[l][None, :], g_conv_ln[l][None, :], b_conv_ln[l][None, :],
                     b, s).reshape(t, CONV_CH)

        gqn = jnp.tile(g_q_nsa[l] * (hd ** -0.5 * LOG2_E), 2)[None, :]
        gkn = jnp.tile(g_k_nsa[l], 2)[None, :]
        qt_n, k_slc, vt_slc, k_win, vt_win, gate_t = _nsa_prep(qn, ks, vs, kw, vw, gn, cos_nsa, sin_nsa, gqn, gkn, b, s)
        wka, wkb, wk2 = _compress_weights(w_cmp_k1[l], w_cmp_k2[l])
        wva, wvb, wv2 = _compress_weights(w_cmp_v1[l], w_cmp_v2[l])
        k_cmp, vt_cmp = _compress(kc, vc, _compress_pe(pe_cmp_k[l]), _compress_pe(pe_cmp_v[l]),
                                  wka, wkb, wk2, wva, wvb, wv2, gkn, cos_cmp, sin_cmp, b, s)
        o_nsa = _nsa_attn(qt_n, k_cmp, vt_cmp, k_slc, vt_slc, k_win, vt_win, gate_t, ovl, b, s).reshape(t, nh * hd)

        x2d = _merge(x2d, o_mla, o_cv, o_nsa, gm, w_o_mla[l].astype(BF16), w_conv_out[l].astype(BF16),
                     b_conv_out[l][None, :], w_o_nsa[l].astype(BF16), w_out[l].astype(BF16))
        x2d = _ffn(x2d, g_ffn[l][None, :], w1=w_ff1[l].astype(BF16), w2=w_ff2[l].astype(BF16))
    return x2d.reshape(b, s, d)
```
